```python
import math
import jax, jax.numpy as jnp
from jax import lax
import numpy as np

D_MODEL = 4096
BATCH = 1
SEQ = 16384
DEPTH = 4

CHUNK = 64
SSM_WIDTH = D_MODEL // 2
POOL_WIDTH = D_MODEL // 2
SSM_GROUP = 16
SSM_GROUPS = SSM_WIDTH // SSM_GROUP
SSM_STATE = 64
POOL_WINDOWS = (2, 4, 8, 16)
POOL_GROUPS = len(POOL_WINDOWS)
POOL_GROUP_WIDTH = POOL_WIDTH // POOL_GROUPS
D_FF = 2 * D_MODEL
CONV_WIDTH = 3
IN_WIDTH = SSM_WIDTH + POOL_WIDTH + 2 * D_MODEL
ALPHA = (2 * DEPTH) ** 0.25
BETA = (8 * DEPTH) ** -0.25
DT_MIN = 1e-3
DT_MAX = 1e-1
LN_EPS = 1e-5

kernel_name = 'hybrid_s5_pool_convffn_deepnorm'


def layer_norm(x, g, b):
    xf = x.astype(jnp.float32)
    mu = jnp.mean(xf, axis=-1, keepdims=True)
    xc = xf - mu
    var = jnp.mean(jnp.square(xc), axis=-1, keepdims=True)
    y = xc * lax.rsqrt(var + LN_EPS) * g.astype(jnp.float32) + b.astype(jnp.float32)
    return y.astype(x.dtype)


def _linear_recurrence_combine(left, right):
    a_l, b_l = left
    a_r, b_r = right
    return a_r * a_l, a_r * b_l + b_r


def s5_mixer(u, lam_re, lam_im, log_dt, b_re, b_im, c_re, c_im, d, w_glu, b_glu):
    bsz, seq, _ = u.shape
    n_chunks = seq // CHUNK
    f32 = jnp.float32
    lam = lax.complex(lam_re.astype(f32), lam_im.astype(f32))
    dt = jnp.exp(log_dt.astype(f32))[:, None]
    lam_dt = lam * dt
    lam_bar = jnp.exp(lam_dt)
    b_c = lax.complex(b_re.astype(f32), b_im.astype(f32))
    b_bar = ((lam_bar - 1.0) / lam)[..., None] * b_c
    ug = u.astype(f32).reshape(bsz, n_chunks, CHUNK, SSM_GROUPS, SSM_GROUP)
    bu = jnp.einsum('bnkgh,gph->bnkgp', ug.astype(jnp.complex64), b_bar)
    a_in = jnp.broadcast_to(lam_bar, bu.shape)
    _, s_loc = lax.associative_scan(_linear_recurrence_combine, (a_in, bu), axis=2)
    ends = s_loc[:, :, -1]
    a_ch = jnp.broadcast_to(jnp.exp(lam_dt * CHUNK), ends.shape)
    _, s_end = lax.associative_scan(_linear_recurrence_combine, (a_ch, ends), axis=1)
    prev = jnp.concatenate([jnp.zeros_like(s_end[:, :1]), s_end[:, :-1]], axis=1)
    powers = jnp.exp(lam_dt[None] * jnp.arange(1, CHUNK + 1, dtype=f32)[:, None, None])
    s = s_loc + powers[None, None] * prev[:, :, None]
    y = (jnp.einsum('bnkgp,ghp->bnkgh', jnp.real(s), c_re.astype(f32))
         - jnp.einsum('bnkgp,ghp->bnkgh', jnp.imag(s), c_im.astype(f32))
         + d.astype(f32) * ug)
    y = y.reshape(bsz, seq, SSM_WIDTH).astype(u.dtype)
    y = jax.nn.gelu(y)
    return y * jax.nn.sigmoid(y @ w_glu + b_glu)


def pool_mixer(u, pool_w, pool_scale):
    bsz, seq, _ = u.shape
    f32 = jnp.float32
    uf = u.astype(f32).reshape(bsz, seq, POOL_GROUPS, POOL_GROUP_WIDTH)
    cs = jnp.cumsum(uf, axis=1)
    cs0 = jnp.pad(cs, ((0, 0), (1, 0), (0, 0), (0, 0)))
    pos = jnp.arange(seq)
    pooled = []
    for gi, w in enumerate(POOL_WINDOWS):
        hi = cs[:, :, gi]
        lo = jnp.pad(cs0[:, :seq + 1 - w, gi], ((0, 0), (w - 1, 0), (0, 0)))
        cnt = jnp.minimum(pos + 1, w).astype(f32)[None, :, None]
        pooled.append((hi - lo) / cnt)
    pooled = jnp.stack(pooled, axis=2) - uf
    mixed = jnp.einsum('blgc,gcd->blgd', pooled, pool_w.astype(f32))
    mixed = mixed.reshape(bsz, seq, POOL_WIDTH) * pool_scale.astype(f32)
    return mixed.astype(u.dtype)


def causal_depthwise_conv(x, w, b):
    out = lax.conv_general_dilated(
        x, w[:, None, :], window_strides=(1,), padding=[(CONV_WIDTH - 1, 0)],
        dimension_numbers=('NWC', 'WIO', 'NWC'), feature_group_count=x.shape[-1])
    return out + b


def setup_inputs(seed: int = 0) -> dict:
    key = jax.random.key(seed)
    ks = jax.random.split(key, 26)
    f32 = jnp.float32
    nrm = lambda k, shape, s: jax.random.normal(k, shape, f32) * s
    n_idx = jnp.arange(SSM_STATE, dtype=f32)
    x = jax.random.normal(ks[0], (BATCH, SEQ, D_MODEL), f32)
    w_in = nrm(ks[1], (DEPTH, D_MODEL, IN_WIDTH), D_MODEL ** -0.5)
    ssm_lam_re = -0.5 + nrm(ks[2], (DEPTH, SSM_GROUPS, SSM_STATE), 0.01)
    ssm_lam_im = math.pi * n_idx + nrm(ks[3], (DEPTH, SSM_GROUPS, SSM_STATE), 0.01)
    ssm_log_dt = jax.random.uniform(ks[4], (DEPTH, SSM_GROUPS), f32,
                                    math.log(DT_MIN), math.log(DT_MAX))
    b_scale = (2.0 * SSM_GROUP) ** -0.5
    ssm_b_re = nrm(ks[5], (DEPTH, SSM_GROUPS, SSM_STATE, SSM_GROUP), b_scale)
    ssm_b_im = nrm(ks[6], (DEPTH, SSM_GROUPS, SSM_STATE, SSM_GROUP), b_scale)
    c_scale = (2.0 * SSM_STATE) ** -0.5
    ssm_c_re = nrm(ks[7], (DEPTH, SSM_GROUPS, SSM_GROUP, SSM_STATE), c_scale)
    ssm_c_im = nrm(ks[8], (DEPTH, SSM_GROUPS, SSM_GROUP, SSM_STATE), c_scale)
    ssm_d = nrm(ks[9], (DEPTH, SSM_GROUPS, SSM_GROUP), 1.0)
    w_glu = nrm(ks[10], (DEPTH, SSM_WIDTH, SSM_WIDTH), SSM_WIDTH ** -0.5)
    b_glu = nrm(ks[11], (DEPTH, SSM_WIDTH), 0.01)
    pool_w = nrm(ks[12], (DEPTH, POOL_GROUPS, POOL_GROUP_WIDTH, POOL_GROUP_WIDTH), POOL_GROUP_WIDTH ** -0.5)
    pool_scale = 1.0 + nrm(ks[13], (DEPTH, POOL_WIDTH), 0.02)
    w_br_ssm = nrm(ks[14], (DEPTH, SSM_WIDTH, D_MODEL), BETA * SSM_WIDTH ** -0.5)
    w_br_pool = nrm(ks[15], (DEPTH, POOL_WIDTH, D_MODEL), BETA * POOL_WIDTH ** -0.5)
    w_out = nrm(ks[16], (DEPTH, D_MODEL, D_MODEL), BETA * D_MODEL ** -0.5)
    ln1_g = 1.0 + nrm(ks[17], (DEPTH, D_MODEL), 0.02)
    ln1_b = nrm(ks[18], (DEPTH, D_MODEL), 0.01)
    w_up = nrm(ks[19], (DEPTH, D_MODEL, 2 * D_FF), D_MODEL ** -0.5)
    conv_w = nrm(ks[20], (DEPTH, CONV_WIDTH, D_FF), CONV_WIDTH ** -0.5)
    conv_b = nrm(ks[21], (DEPTH, D_FF), 0.01)
    w_down = nrm(ks[22], (DEPTH, D_FF, D_MODEL), BETA * D_FF ** -0.5)
    ln2_g = 1.0 + nrm(ks[23], (DEPTH, D_MODEL), 0.02)
    ln2_b = nrm(ks[24], (DEPTH, D_MODEL), 0.01)
    return {'x': x, 'w_in': w_in, 'ssm_lam_re': ssm_lam_re, 'ssm_lam_im': ssm_lam_im,
            'ssm_log_dt': ssm_log_dt, 'ssm_b_re': ssm_b_re, 'ssm_b_im': ssm_b_im,
            'ssm_c_re': ssm_c_re, 'ssm_c_im': ssm_c_im, 'ssm_d': ssm_d,
            'w_glu': w_glu, 'b_glu': b_glu, 'pool_w': pool_w, 'pool_scale': pool_scale,
            'w_br_ssm': w_br_ssm, 'w_br_pool': w_br_pool, 'w_out': w_out,
            'ln1_g': ln1_g, 'ln1_b': ln1_b, 'w_up': w_up, 'conv_w': conv_w, 'conv_b': conv_b,
            'w_down': w_down, 'ln2_g': ln2_g, 'ln2_b': ln2_b}


def reference(x, w_in, ssm_lam_re, ssm_lam_im, ssm_log_dt, ssm_b_re, ssm_b_im,
              ssm_c_re, ssm_c_im, ssm_d, w_glu, b_glu, pool_w, pool_scale,
              w_br_ssm, w_br_pool, w_out, ln1_g, ln1_b, w_up, conv_w, conv_b,
              w_down, ln2_g, ln2_b):
    h = x
    splits = [SSM_WIDTH, SSM_WIDTH + POOL_WIDTH, SSM_WIDTH + POOL_WIDTH + D_MODEL]
    for i in range(DEPTH):
        proj = h @ w_in[i]
        u_ssm, u_pool, g_ssm, g_pool = jnp.split(proj, splits, axis=-1)
        y_ssm = s5_mixer(u_ssm, ssm_lam_re[i], ssm_lam_im[i], ssm_log_dt[i],
                         ssm_b_re[i], ssm_b_im[i], ssm_c_re[i], ssm_c_im[i], ssm_d[i],
                         w_glu[i], b_glu[i]) @ w_br_ssm[i]
        y_pool = pool_mixer(u_pool, pool_w[i], pool_scale[i]) @ w_br_pool[i]
        merged = jax.nn.sigmoid(g_ssm) * y_ssm + jax.nn.sigmoid(g_pool) * y_pool
        h = layer_norm(ALPHA * h + merged @ w_out[i], ln1_g[i], ln1_b[i])
        up = h @ w_up[i]
        gate, val = jnp.split(up, [D_FF], axis=-1)
        act = jax.nn.gelu(causal_depthwise_conv(gate, conv_w[i], conv_b[i])) * val
        h = layer_norm(ALPHA * h + act @ w_down[i], ln2_g[i], ln2_b[i])
    return h
```

```python
import functools
import math

import numpy as np
import jax
import jax.numpy as jnp
from jax import lax
from jax.experimental import pallas as pl
from jax.experimental.pallas import tpu as pltpu

F32 = jnp.float32
BF16 = jnp.bfloat16

SSM_GROUP = 16
SSM_STATE = 64
POOL_WINDOWS = (2, 4, 8, 16)
CONV_WIDTH = 3
LN_EPS = 1e-5

LANES = 128
T_CHUNK = 16
GROUPS_PER_TILE = LANES // SSM_GROUP
HALF_T = LANES // SSM_GROUP
GROUP_COLS = T_CHUNK * SSM_GROUP
POOL_HALO = 16
VMEM_LIMIT = 56 * 1024 * 1024

ROW_BLOCK = 1024
PROJ_COLS = 1024
EPI_COLS = 512
DOWN_ROWS = 512
S5_CHUNK_ROWS = 256
SCAN_COLS = 512
POOL_ROWS = 512
LN_ROWS = 256


def _gelu(x):
    c = math.sqrt(2.0 / math.pi)
    return x * (0.5 * (1.0 + jnp.tanh(c * (x + 0.044715 * (x * x * x)))))


def _sigmoid(x):
    return 1.0 / (1.0 + jnp.exp(-x))


def _params(*sem):
    return pltpu.CompilerParams(dimension_semantics=sem, vmem_limit_bytes=VMEM_LIMIT)


def _dot(a, b):
    return jnp.dot(a, b, preferred_element_type=F32)


def _proj_kernel(x_ref, w_ref, o_ref):
    o_ref[...] = _dot(x_ref[...], w_ref[...])


def _proj(hb, w_in_b, layer):
    L, D = hb.shape
    n_out = w_in_b.shape[2]
    bm, bn = min(ROW_BLOCK, L), min(PROJ_COLS, n_out)
    return pl.pallas_call(
        _proj_kernel,
        grid=(L // bm, n_out // bn),
        in_specs=[pl.BlockSpec((bm, D), lambda i, j: (i, 0)),
                  pl.BlockSpec((None, D, bn), lambda i, j: (layer, 0, j))],
        out_specs=pl.BlockSpec((bm, bn), lambda i, j: (i, j)),
        out_shape=jax.ShapeDtypeStruct((L, n_out), F32),
        compiler_params=_params("parallel", "parallel"),
        name="proj",
    )(hb, w_in_b)


def _regroup_matrix():
    n = HALF_T * LANES
    src = np.arange(n)
    t8, g8, h = src // LANES, (src % LANES) // SSM_GROUP, src % SSM_GROUP
    dst = g8 * LANES + t8 * SSM_GROUP + h
    m = np.zeros((n, n), np.float32)
    m[src, dst] = 1.0
    return m


def _s5_in_kernel(u_ref, perm_ref, bp_ref, z_ref, sre_ref, sim_ref, *, bc):
    halves = []
    for j in range(T_CHUNK // HALF_T):
        xcat = jnp.concatenate(
            [u_ref[pl.ds(HALF_T * j + t8, bc, stride=T_CHUNK), :].astype(BF16) for t8 in range(HALF_T)],
            axis=1)
        halves.append(_dot(xcat, perm_ref[...]).astype(BF16))
    for g8 in range(GROUPS_PER_TILE):
        for j, half in enumerate(halves):
            z_ref[:, GROUP_COLS * g8 + LANES * j:GROUP_COLS * g8 + LANES * (j + 1)] = (
                half[:, LANES * g8:LANES * (g8 + 1)])
    for q in range(GROUPS_PER_TILE // 2):
        s = _dot(z_ref[:, 2 * GROUP_COLS * q:2 * GROUP_COLS * (q + 1)], bp_ref[q])
        sre_ref[:, LANES * q:LANES * (q + 1)] = s[:, :LANES]
        sim_ref[:, LANES * q:LANES * (q + 1)] = s[:, LANES:]


def _s5_scan_kernel(sre_ref, sim_ref, are_ref, aim_ref, pre_ref, pim_ref):
    n_chunks, width = sre_ref.shape
    ar = are_ref[...]
    ai = aim_ref[...]

    def body(r, carry):
        sr, si = carry
        row = pl.ds(r, 1)
        pre_ref[row, :] = sr
        pim_ref[row, :] = si
        nr = ar * sr - ai * si + sre_ref[row, :]
        ni = ar * si + ai * sr + sim_ref[row, :]
        return nr, ni

    zero = jnp.zeros((1, width), F32)
    lax.fori_loop(0, n_chunks, body, (zero, zero))


def _s5_out_kernel(z_ref, pre_ref, pim_ref, a_ref, cp_ref, permt_ref, u_ref, d_ref,
                   yf_ref, yb_ref, ybuf_ref, *, bc):
    ys = []
    for q in range(GROUPS_PER_TILE // 2):
        state = jnp.concatenate([pre_ref[:, LANES * q:LANES * (q + 1)],
                                 pim_ref[:, LANES * q:LANES * (q + 1)]], axis=1).astype(BF16)
        carried = _dot(state, cp_ref[q])
        for e in range(2):
            g8 = 2 * q + e
            local = _dot(z_ref[:, GROUP_COLS * g8:GROUP_COLS * (g8 + 1)], a_ref[g8])
            ys.append(local + carried[:, GROUP_COLS * e:GROUP_COLS * (e + 1)])
    for j in range(T_CHUNK // HALF_T):
        ycat = jnp.concatenate([y[:, LANES * j:LANES * (j + 1)] for y in ys], axis=1)
        hi = ycat.astype(BF16)
        lo = (ycat - hi.astype(F32)).astype(BF16)
        yt = _dot(hi, permt_ref[...]) + _dot(lo, permt_ref[...])
        for t8 in range(HALF_T):
            ybuf_ref[pl.ds(HALF_T * j + t8, bc, stride=T_CHUNK), :] = yt[:, LANES * t8:LANES * (t8 + 1)]
    y = _gelu(ybuf_ref[...] + d_ref[...] * u_ref[...])
    yf_ref[...] = y
    yb_ref[...] = y.astype(BF16)


def _s5_matrices(lam_re, lam_im, log_dt, b_re, b_im, c_re, c_im):
    hp = lax.Precision.HIGHEST
    depth, n_groups, n_state = lam_re.shape
    dt = jnp.exp(log_dt)[..., None]
    zr, zi = lam_re * dt, lam_im * dt
    steps = jnp.arange(T_CHUNK + 1, dtype=F32)[None, :, None, None]
    mag = jnp.exp(zr[:, None] * steps)
    pw_re = mag * jnp.cos(zi[:, None] * steps)
    pw_im = mag * jnp.sin(zi[:, None] * steps)
    e1_re = jnp.expm1(zr) * jnp.cos(zi) - 2.0 * jnp.square(jnp.sin(0.5 * zi))
    e1_im = jnp.exp(zr) * jnp.sin(zi)
    den = lam_re * lam_re + lam_im * lam_im
    f_re = (e1_re * lam_re + e1_im * lam_im) / den
    f_im = (e1_im * lam_re - e1_re * lam_im) / den
    bb_re = f_re[..., None] * b_re - f_im[..., None] * b_im
    bb_im = f_re[..., None] * b_im + f_im[..., None] * b_re
    cw_re = c_re[:, None] * pw_re[:, :, :, None, :] - c_im[:, None] * pw_im[:, :, :, None, :]
    cw_im = c_re[:, None] * pw_im[:, :, :, None, :] + c_im[:, None] * pw_re[:, :, :, None, :]
    kern = (jnp.einsum('ldgop,lgpi->ldgoi', cw_re[:, :T_CHUNK], bb_re, precision=hp)
            - jnp.einsum('ldgop,lgpi->ldgoi', cw_im[:, :T_CHUNK], bb_im, precision=hp))
    t_in = np.arange(T_CHUNK)[:, None]
    t_out = np.arange(T_CHUNK)[None, :]
    lag = t_out - t_in
    a_mat = jnp.where((lag >= 0)[None, :, :, None, None, None],
                      kern[:, np.clip(lag, 0, T_CHUNK - 1)], 0.0)
    a_mat = jnp.transpose(a_mat, (0, 3, 1, 5, 2, 4)).reshape(depth, n_groups, GROUP_COLS, GROUP_COLS)
    rev_re = pw_re[:, T_CHUNK - 1::-1][:, :T_CHUNK]
    rev_im = pw_im[:, T_CHUNK - 1::-1][:, :T_CHUNK]
    bp_re = rev_re[..., None] * bb_re[:, None] - rev_im[..., None] * bb_im[:, None]
    bp_im = rev_re[..., None] * bb_im[:, None] + rev_im[..., None] * bb_re[:, None]

    def rows_th(m):
        return jnp.transpose(m, (0, 2, 1, 4, 3)).reshape(depth, n_groups, GROUP_COLS, n_state)

    bp_re, bp_im = rows_th(bp_re), rows_th(bp_im)
    zeros = jnp.zeros_like(bp_re[:, 0::2])
    top = jnp.concatenate([bp_re[:, 0::2], zeros, bp_im[:, 0::2], zeros], axis=-1)
    bot = jnp.concatenate([zeros, bp_re[:, 1::2], zeros, bp_im[:, 1::2]], axis=-1)
    bp = jnp.concatenate([top, bot], axis=-2)
    def cols_to(m):
        return jnp.transpose(m, (0, 2, 4, 1, 3)).reshape(depth, n_groups, n_state, GROUP_COLS)

    co_re, co_im = cols_to(cw_re[:, 1:]), cols_to(-cw_im[:, 1:])
    zc = jnp.zeros_like(co_re[:, 0::2])
    cp = jnp.concatenate([
        jnp.concatenate([co_re[:, 0::2], zc], axis=-1),
        jnp.concatenate([zc, co_re[:, 1::2]], axis=-1),
        jnp.concatenate([co_im[:, 0::2], zc], axis=-1),
        jnp.concatenate([zc, co_im[:, 1::2]], axis=-1)], axis=-2)
    a_re = pw_re[:, T_CHUNK].reshape(depth, 1, n_groups // 2, 2 * n_state).reshape(depth, 1, -1)
    a_im = pw_im[:, T_CHUNK].reshape(depth, 1, -1)
    return a_mat.astype(BF16), bp.astype(BF16), cp.astype(BF16), a_re, a_im


def _s5(proj, mats, ssm_d, layer, ssm_width):
    a_mat, bp, cp, a_re, a_im = mats
    L = proj.shape[0]
    n_groups = ssm_width // SSM_GROUP
    n_tiles = ssm_width // LANES
    n_chunks = L // T_CHUNK
    bc = min(S5_CHUNK_ROWS, n_chunks)
    bm = bc * T_CHUNK
    pairs = GROUPS_PER_TILE // 2
    state_w = n_groups * SSM_STATE
    perm = jnp.asarray(_regroup_matrix(), BF16)
    permt = jnp.asarray(_regroup_matrix().T, BF16)
    n_perm = perm.shape[0]

    z, s_re, s_im = pl.pallas_call(
        functools.partial(_s5_in_kernel, bc=bc),
        grid=(n_tiles, n_chunks // bc),
        in_specs=[pl.BlockSpec((bm, LANES), lambda v, r: (r, v)),
                  pl.BlockSpec((n_perm, n_perm), lambda v, r: (0, 0)),
                  pl.BlockSpec((None, pairs, 2 * GROUP_COLS, 4 * SSM_STATE), lambda v, r: (layer, v, 0, 0))],
        out_specs=[pl.BlockSpec((bc, GROUPS_PER_TILE * GROUP_COLS), lambda v, r: (r, v)),
                   pl.BlockSpec((bc, pairs * LANES), lambda v, r: (r, v)),
                   pl.BlockSpec((bc, pairs * LANES), lambda v, r: (r, v))],
        out_shape=[jax.ShapeDtypeStruct((n_chunks, n_groups * GROUP_COLS), BF16),
                   jax.ShapeDtypeStruct((n_chunks, state_w), F32),
                   jax.ShapeDtypeStruct((n_chunks, state_w), F32)],
        compiler_params=_params("parallel", "parallel"),
        name="s5_in",
    )(proj, perm, bp)

    bw = min(SCAN_COLS, state_w)
    p_re, p_im = pl.pallas_call(
        _s5_scan_kernel,
        grid=(state_w // bw,),
        in_specs=[pl.BlockSpec((n_chunks, bw), lambda c: (0, c)),
                  pl.BlockSpec((n_chunks, bw), lambda c: (0, c)),
                  pl.BlockSpec((None, 1, bw), lambda c: (layer, 0, c)),
                  pl.BlockSpec((None, 1, bw), lambda c: (layer, 0, c))],
        out_specs=[pl.BlockSpec((n_chunks, bw), lambda c: (0, c)),
                   pl.BlockSpec((n_chunks, bw), lambda c: (0, c))],
        out_shape=[jax.ShapeDtypeStruct((n_chunks, state_w), F32),
                   jax.ShapeDtypeStruct((n_chunks, state_w), F32)],
        compiler_params=_params("parallel"),
        name="s5_scan",
    )(s_re, s_im, a_re, a_im)

    yf, yb = pl.pallas_call(
        functools.partial(_s5_out_kernel, bc=bc),
        grid=(n_tiles, n_chunks // bc),
        in_specs=[pl.BlockSpec((bc, GROUPS_PER_TILE * GROUP_COLS), lambda v, r: (r, v)),
                  pl.BlockSpec((bc, pairs * LANES), lambda v, r: (r, v)),
                  pl.BlockSpec((bc, pairs * LANES), lambda v, r: (r, v)),
                  pl.BlockSpec((None, GROUPS_PER_TILE, GROUP_COLS, GROUP_COLS), lambda v, r: (layer, v, 0, 0)),
                  pl.BlockSpec((None, pairs, 4 * SSM_STATE, 2 * GROUP_COLS), lambda v, r: (layer, v, 0, 0)),
                  pl.BlockSpec((n_perm, n_perm), lambda v, r: (0, 0)),
                  pl.BlockSpec((bm, LANES), lambda v, r: (r, v)),
                  pl.BlockSpec((None, 1, LANES), lambda v, r: (layer, 0, v))],
        out_specs=[pl.BlockSpec((bm, LANES), lambda v, r: (r, v)),
                   pl.BlockSpec((bm, LANES), lambda v, r: (r, v))],
        out_shape=[jax.ShapeDtypeStruct((L, ssm_width), F32),
                   jax.ShapeDtypeStruct((L, ssm_width), BF16)],
        scratch_shapes=[pltpu.VMEM((bm, LANES), F32)],
        compiler_params=_params("parallel", "parallel"),
        name="s5_out",
    )(z, p_re, p_im, a_mat, cp, permt, proj, ssm_d)
    return yf, yb


def _glu_kernel(yb_ref, w_ref, b_ref, yf_ref, o_ref):
    gate = _dot(yb_ref[...], w_ref[...]) + b_ref[...]
    o_ref[...] = (yf_ref[...] * _sigmoid(gate)).astype(BF16)


def _glu(yf, yb, w_glu_b, b_glu, layer):
    L, W = yf.shape
    bm, bn = min(ROW_BLOCK, L), min(EPI_COLS, W)
    return pl.pallas_call(
        _glu_kernel,
        grid=(L // bm, W // bn),
        in_specs=[pl.BlockSpec((bm, W), lambda i, j: (i, 0)),
                  pl.BlockSpec((None, W, bn), lambda i, j: (layer, 0, j)),
                  pl.BlockSpec((None, 1, bn), lambda i, j: (layer, 0, j)),
                  pl.BlockSpec((bm, bn), lambda i, j: (i, j))],
        out_specs=pl.BlockSpec((bm, bn), lambda i, j: (i, j)),
        out_shape=jax.ShapeDtypeStruct((L, W), BF16),
        compiler_params=_params("parallel", "parallel"),
        name="glu",
    )(yb, w_glu_b, b_glu, yf)


def _pool_kernel(x_ref, prev_ref, w_ref, sc_ref, o_ref, xe_ref, *, bm, gw):
    i = pl.program_id(0)

    @pl.when(i == 0)
    def _():
        xe_ref[0:POOL_HALO, :] = jnp.zeros((POOL_HALO, xe_ref.shape[1]), F32)

    @pl.when(i > 0)
    def _():
        xe_ref[0:POOL_HALO, :] = prev_ref[...]

    xe_ref[POOL_HALO:POOL_HALO + bm, :] = x_ref[...]
    pos = i * bm + lax.broadcasted_iota(jnp.int32, (bm, 1), 0)
    for gi, win in enumerate(POOL_WINDOWS):
        cols = slice(gw * gi, gw * (gi + 1))
        x = x_ref[:, cols]
        acc = x
        for k in range(1, win):
            acc = acc + xe_ref[POOL_HALO - k:POOL_HALO - k + bm, cols]
        cnt = jnp.minimum(pos + 1, win).astype(F32)
        pooled = acc / cnt - x
        mixed = _dot(pooled.astype(BF16), w_ref[gi]) * sc_ref[:, cols]
        o_ref[:, cols] = mixed.astype(BF16)


def _pool(proj, pool_w_b, pool_scale, layer, col0, width):
    L = proj.shape[0]
    gw = width // len(POOL_WINDOWS)
    bm = min(POOL_ROWS, L)
    assert col0 % width == 0 and bm % POOL_HALO == 0
    cb = col0 // width
    halo_blocks = bm // POOL_HALO
    return pl.pallas_call(
        functools.partial(_pool_kernel, bm=bm, gw=gw),
        grid=(L // bm,),
        in_specs=[pl.BlockSpec((bm, width), lambda i: (i, cb)),
                  pl.BlockSpec((POOL_HALO, width), lambda i: (jnp.maximum(i * halo_blocks - 1, 0), cb)),
                  pl.BlockSpec((None, len(POOL_WINDOWS), gw, gw), lambda i: (layer, 0, 0, 0)),
                  pl.BlockSpec((None, 1, width), lambda i: (layer, 0, 0))],
        out_specs=pl.BlockSpec((bm, width), lambda i: (i, 0)),
        out_shape=jax.ShapeDtypeStruct((L, width), BF16),
        scratch_shapes=[pltpu.VMEM((bm + POOL_HALO, width), F32)],
        compiler_params=_params("parallel"),
        name="pool",
    )(proj, proj, pool_w_b, pool_scale)


def _merge_kernel(a_ref, b_ref, wa_ref, wb_ref, ga_ref, gb_ref, o_ref):
    ya = _dot(a_ref[...], wa_ref[...])
    yb = _dot(b_ref[...], wb_ref[...])
    o_ref[...] = (_sigmoid(ga_ref[...]) * ya + _sigmoid(gb_ref[...]) * yb).astype(BF16)


def _merge(za, zb, w_a, w_b, proj, layer, gate_col0):
    L, W = za.shape
    D = w_a.shape[2]
    bm, bn = min(ROW_BLOCK, L), min(EPI_COLS, D)
    assert gate_col0 % bn == 0 and D % bn == 0
    ga0 = gate_col0 // bn
    gb0 = (gate_col0 + D) // bn
    return pl.pallas_call(
        _merge_kernel,
        grid=(L // bm, D // bn),
        in_specs=[pl.BlockSpec((bm, W), lambda i, j: (i, 0)),
                  pl.BlockSpec((bm, W), lambda i, j: (i, 0)),
                  pl.BlockSpec((None, W, bn), lambda i, j: (layer, 0, j)),
                  pl.BlockSpec((None, W, bn), lambda i, j: (layer, 0, j)),
                  pl.BlockSpec((bm, bn), lambda i, j: (i, ga0 + j)),
                  pl.BlockSpec((bm, bn), lambda i, j: (i, gb0 + j))],
        out_specs=pl.BlockSpec((bm, bn), lambda i, j: (i, j)),
        out_shape=jax.ShapeDtypeStruct((L, D), BF16),
        compiler_params=_params("parallel", "parallel"),
        name="merge",
    )(za, zb, w_a, w_b, proj, proj)


def _resid_kernel(x_ref, w_ref, h_ref, o_ref, *, alpha):
    o_ref[...] = alpha * h_ref[...] + _dot(x_ref[...], w_ref[...])


def _resid_matmul(xb, w_b, h, layer, alpha, bm, name):
    L, K = xb.shape
    D = w_b.shape[2]
    bm, bn = min(bm, L), min(EPI_COLS, D)
    return pl.pallas_call(
        functools.partial(_resid_kernel, alpha=alpha),
        grid=(L // bm, D // bn),
        in_specs=[pl.BlockSpec((bm, K), lambda i, j: (i, 0)),
                  pl.BlockSpec((None, K, bn), lambda i, j: (layer, 0, j)),
                  pl.BlockSpec((bm, bn), lambda i, j: (i, j))],
        out_specs=pl.BlockSpec((bm, bn), lambda i, j: (i, j)),
        out_shape=jax.ShapeDtypeStruct((L, D), F32),
        compiler_params=_params("parallel", "parallel"),
        name=name,
    )(xb, w_b, h)


def _ln_kernel(x_ref, g_ref, b_ref, of_ref, ob_ref):
    x = x_ref[...]
    mu = jnp.mean(x, axis=-1, keepdims=True)
    xc = x - mu
    var = jnp.mean(xc * xc, axis=-1, keepdims=True)
    y = xc * lax.rsqrt(var + LN_EPS) * g_ref[...] + b_ref[...]
    of_ref[...] = y
    ob_ref[...] = y.astype(BF16)


def _layer_norm(x, g, b, layer):
    L, D = x.shape
    bm = min(LN_ROWS, L)
    return pl.pallas_call(
        _ln_kernel,
        grid=(L // bm,),
        in_specs=[pl.BlockSpec((bm, D), lambda i: (i, 0)),
                  pl.BlockSpec((None, 1, D), lambda i: (layer, 0, 0)),
                  pl.BlockSpec((None, 1, D), lambda i: (layer, 0, 0))],
        out_specs=[pl.BlockSpec((bm, D), lambda i: (i, 0)),
                   pl.BlockSpec((bm, D), lambda i: (i, 0))],
        out_shape=[jax.ShapeDtypeStruct((L, D), F32), jax.ShapeDtypeStruct((L, D), BF16)],
        compiler_params=_params("parallel"),
        name="layer_norm",
    )(x, g, b)


CONV_PAD = 8


def _up_kernel(x_ref, wg_ref, wv_ref, cw_ref, cb_ref, o_ref, gbuf_ref, *, bm):
    i = pl.program_id(1)
    x = x_ref[...]
    gate = _dot(x, wg_ref[...])
    val = _dot(x, wv_ref[...])

    @pl.when(i == 0)
    def _():
        gbuf_ref[0:CONV_PAD, :] = jnp.zeros((CONV_PAD, gbuf_ref.shape[1]), F32)

    @pl.when(i > 0)
    def _():
        gbuf_ref[0:CONV_PAD, :] = gbuf_ref[bm:bm + CONV_PAD, :]

    gbuf_ref[CONV_PAD:CONV_PAD + bm, :] = gate
    conv = cw_ref[2:3, :] * gate + cb_ref[...]
    conv = conv + cw_ref[1:2, :] * gbuf_ref[CONV_PAD - 1:CONV_PAD - 1 + bm, :]
    conv = conv + cw_ref[0:1, :] * gbuf_ref[CONV_PAD - 2:CONV_PAD - 2 + bm, :]
    o_ref[...] = (_gelu(conv) * val).astype(BF16)


def _up(hb, w_up_b, conv_w, conv_b, layer):
    L, D = hb.shape
    d_ff = w_up_b.shape[2] // 2
    bm, bn = min(ROW_BLOCK, L), min(EPI_COLS, d_ff)
    nb = d_ff // bn
    return pl.pallas_call(
        functools.partial(_up_kernel, bm=bm),
        grid=(nb, L // bm),
        in_specs=[pl.BlockSpec((bm, D), lambda j, i: (i, 0)),
                  pl.BlockSpec((None, D, bn), lambda j, i: (layer, 0, j)),
                  pl.BlockSpec((None, D, bn), lambda j, i: (layer, 0, nb + j)),
                  pl.BlockSpec((None, CONV_WIDTH, bn), lambda j, i: (layer, 0, j)),
                  pl.BlockSpec((None, 1, bn), lambda j, i: (layer, 0, j))],
        out_specs=pl.BlockSpec((bm, bn), lambda j, i: (i, j)),
        out_shape=jax.ShapeDtypeStruct((L, d_ff), BF16),
        scratch_shapes=[pltpu.VMEM((bm + CONV_PAD, bn), F32)],
        compiler_params=_params("arbitrary", "arbitrary"),
        name="up",
    )(hb, w_up_b, w_up_b, conv_w, conv_b)


def kernel(x, w_in, ssm_lam_re, ssm_lam_im, ssm_log_dt, ssm_b_re, ssm_b_im, ssm_c_re, ssm_c_im, ssm_d, w_glu, b_glu, pool_w, pool_scale, w_br_ssm, w_br_pool, w_out, ln1_g, ln1_b, w_up, conv_w, conv_b, w_down, ln2_g, ln2_b):
    bsz, seq, d_model = x.shape
    depth = w_in.shape[0]
    ssm_width = w_glu.shape[1]
    pool_width = pool_scale.shape[1]
    alpha = (2 * depth) ** 0.25
    assert bsz == 1 and seq % (T_CHUNK * 8) == 0 and ssm_width % LANES == 0

    w_in_b, w_glu_b, pool_w_b = w_in.astype(BF16), w_glu.astype(BF16), pool_w.astype(BF16)
    w_br_ssm_b, w_br_pool_b, w_out_b = w_br_ssm.astype(BF16), w_br_pool.astype(BF16), w_out.astype(BF16)
    w_up_b, w_down_b = w_up.astype(BF16), w_down.astype(BF16)
    mats = _s5_matrices(ssm_lam_re, ssm_lam_im, ssm_log_dt, ssm_b_re, ssm_b_im, ssm_c_re, ssm_c_im)
    row = lambda a: a.reshape(depth, 1, -1)
    ssm_d_r, b_glu_r, pool_scale_r, conv_b_r = row(ssm_d), row(b_glu), row(pool_scale), row(conv_b)
    ln1_g_r, ln1_b_r, ln2_g_r, ln2_b_r = row(ln1_g), row(ln1_b), row(ln2_g), row(ln2_b)

    h = x.reshape(seq, d_model)
    hb = h.astype(BF16)
    for layer in range(depth):
        proj = _proj(hb, w_in_b, layer)
        yf, yb = _s5(proj, mats, ssm_d_r, layer, ssm_width)
        za = _glu(yf, yb, w_glu_b, b_glu_r, layer)
        zb = _pool(proj, pool_w_b, pool_scale_r, layer, ssm_width, pool_width)
        merged = _merge(za, zb, w_br_ssm_b, w_br_pool_b, proj, layer, ssm_width + pool_width)
        pre = _resid_matmul(merged, w_out_b, h, layer, alpha, ROW_BLOCK, "wout")
        h, hb = _layer_norm(pre, ln1_g_r, ln1_b_r, layer)
        act = _up(hb, w_up_b, conv_w, conv_b_r, layer)
        pre = _resid_matmul(act, w_down_b, h, layer, alpha, DOWN_ROWS, "down")
        h, hb = _layer_norm(pre, ln2_g_r, ln2_b_r, layer)
    return h.reshape(bsz, seq, d_model)
```

```python
import functools
import math

import numpy as np
import jax
import jax.numpy as jnp
from jax import lax
from jax.experimental import pallas as pl
from jax.experimental.pallas import tpu as pltpu

F32 = jnp.float32
BF16 = jnp.bfloat16

SSM_GROUP = 16
SSM_STATE = 64
POOL_WINDOWS = (2, 4, 8, 16)
CONV_WIDTH = 3
LN_EPS = 1e-5

LANES = 128
T_CHUNK = 16
GROUPS_PER_TILE = LANES // SSM_GROUP
HALF_T = LANES // SSM_GROUP
GROUP_COLS = T_CHUNK * SSM_GROUP
POOL_HALO = 16
VMEM_LIMIT = 56 * 1024 * 1024
LN_FUSED_VMEM_LIMIT = 60 * 1024 * 1024

ROW_BLOCK = 1024
PROJ_COLS = 1024
EPI_COLS = 512
DOWN_ROWS = 512
UP_ROWS = 512
UP_COLS = 1024
MXU_COLS = 256
S5_CHUNK_ROWS = 256
SCAN_COLS = 512
POOL_ROWS = 512
LN_ROWS = 256
LN_FUSED_ROWS = 512


def _gelu(x):
    c = math.sqrt(2.0 / math.pi)
    return x * (0.5 * (1.0 + jnp.tanh(c * (x + 0.044715 * (x * x * x)))))


def _sigmoid(x):
    return 1.0 / (1.0 + jnp.exp(-x))


def _params(*sem, vmem=VMEM_LIMIT):
    return pltpu.CompilerParams(dimension_semantics=sem, vmem_limit_bytes=vmem)


def _dot(a, b):
    return jnp.dot(a, b, preferred_element_type=F32)


def _proj_kernel(x_ref, w_ref, o_ref):
    o_ref[...] = _dot(x_ref[...], w_ref[...])


def _proj(hb, w_in_b, layer):
    L, D = hb.shape
    n_out = w_in_b.shape[2]
    bm, bn = min(ROW_BLOCK, L), min(PROJ_COLS, n_out)
    return pl.pallas_call(
        _proj_kernel,
        grid=(L // bm, n_out // bn),
        in_specs=[pl.BlockSpec((bm, D), lambda i, j: (i, 0)),
                  pl.BlockSpec((None, D, bn), lambda i, j: (layer, 0, j))],
        out_specs=pl.BlockSpec((bm, bn), lambda i, j: (i, j)),
        out_shape=jax.ShapeDtypeStruct((L, n_out), F32),
        compiler_params=_params("parallel", "parallel"),
        name="proj",
    )(hb, w_in_b)


def _regroup_matrix():
    n = HALF_T * LANES
    src = np.arange(n)
    t8, g8, h = src // LANES, (src % LANES) // SSM_GROUP, src % SSM_GROUP
    dst = g8 * LANES + t8 * SSM_GROUP + h
    m = np.zeros((n, n), np.float32)
    m[src, dst] = 1.0
    return m


def _s5_in_kernel(u_ref, perm_ref, bp_ref, z_ref, sre_ref, sim_ref, *, bc):
    halves = []
    for j in range(T_CHUNK // HALF_T):
        xcat = jnp.concatenate(
            [u_ref[pl.ds(HALF_T * j + t8, bc, stride=T_CHUNK), :].astype(BF16) for t8 in range(HALF_T)],
            axis=1)
        halves.append(_dot(xcat, perm_ref[...]).astype(BF16))
    for g8 in range(GROUPS_PER_TILE):
        for j, half in enumerate(halves):
            z_ref[:, GROUP_COLS * g8 + LANES * j:GROUP_COLS * g8 + LANES * (j + 1)] = (
                half[:, LANES * g8:LANES * (g8 + 1)])
    for q in range(GROUPS_PER_TILE // 2):
        s = _dot(z_ref[:, 2 * GROUP_COLS * q:2 * GROUP_COLS * (q + 1)], bp_ref[q])
        sre_ref[:, LANES * q:LANES * (q + 1)] = s[:, :LANES]
        sim_ref[:, LANES * q:LANES * (q + 1)] = s[:, LANES:]


def _s5_scan_kernel(sre_ref, sim_ref, are_ref, aim_ref, pre_ref, pim_ref):
    n_chunks, width = sre_ref.shape
    ar = are_ref[...]
    ai = aim_ref[...]

    def body(r, carry):
        sr, si = carry
        row = pl.ds(r, 1)
        pre_ref[row, :] = sr
        pim_ref[row, :] = si
        nr = ar * sr - ai * si + sre_ref[row, :]
        ni = ar * si + ai * sr + sim_ref[row, :]
        return nr, ni

    zero = jnp.zeros((1, width), F32)
    lax.fori_loop(0, n_chunks, body, (zero, zero))


def _s5_out_kernel(z_ref, pre_ref, pim_ref, a_ref, cp_ref, permt_ref, u_ref, d_ref,
                   yf_ref, yb_ref, ybuf_ref, *, bc):
    ys = []
    for q in range(GROUPS_PER_TILE // 2):
        state = jnp.concatenate([pre_ref[:, LANES * q:LANES * (q + 1)],
                                 pim_ref[:, LANES * q:LANES * (q + 1)]], axis=1).astype(BF16)
        carried = _dot(state, cp_ref[q])
        for e in range(2):
            g8 = 2 * q + e
            local = _dot(z_ref[:, GROUP_COLS * g8:GROUP_COLS * (g8 + 1)], a_ref[g8])
            ys.append(local + carried[:, GROUP_COLS * e:GROUP_COLS * (e + 1)])
    for j in range(T_CHUNK // HALF_T):
        ycat = jnp.concatenate([y[:, LANES * j:LANES * (j + 1)] for y in ys], axis=1)
        yt = _dot(ycat.astype(BF16), permt_ref[...])
        for t8 in range(HALF_T):
            ybuf_ref[pl.ds(HALF_T * j + t8, bc, stride=T_CHUNK), :] = yt[:, LANES * t8:LANES * (t8 + 1)]
    y = _gelu(ybuf_ref[...] + d_ref[...] * u_ref[...])
    yf_ref[...] = y
    yb_ref[...] = y.astype(BF16)


def _s5_matrices(lam_re, lam_im, log_dt, b_re, b_im, c_re, c_im):
    hp = lax.Precision.HIGHEST
    depth, n_groups, n_state = lam_re.shape
    dt = jnp.exp(log_dt)[..., None]
    zr, zi = lam_re * dt, lam_im * dt
    steps = jnp.arange(T_CHUNK + 1, dtype=F32)
    mag = jnp.exp(zr[..., None] * steps)
    pw_re = mag * jnp.cos(zi[..., None] * steps)
    pw_im = mag * jnp.sin(zi[..., None] * steps)
    e1_re = jnp.expm1(zr) * jnp.cos(zi) - 2.0 * jnp.square(jnp.sin(0.5 * zi))
    e1_im = jnp.exp(zr) * jnp.sin(zi)
    den = lam_re * lam_re + lam_im * lam_im
    f_re = (e1_re * lam_re + e1_im * lam_im) / den
    f_im = (e1_im * lam_re - e1_re * lam_im) / den
    bb_re = f_re[..., None] * b_re - f_im[..., None] * b_im
    bb_im = f_re[..., None] * b_im + f_im[..., None] * b_re
    ct_re, ct_im = jnp.swapaxes(c_re, 2, 3)[:, :, :, None, :], jnp.swapaxes(c_im, 2, 3)[:, :, :, None, :]
    cw_re = ct_re * pw_re[..., None] - ct_im * pw_im[..., None]
    cw_im = ct_re * pw_im[..., None] + ct_im * pw_re[..., None]
    flat = lambda m: m.reshape(depth, n_groups, n_state, GROUP_COLS)
    kern = (jnp.einsum('lgpi,lgpn->lgin', bb_re, flat(cw_re[:, :, :, :T_CHUNK]), precision=hp)
            - jnp.einsum('lgpi,lgpn->lgin', bb_im, flat(cw_im[:, :, :, :T_CHUNK]), precision=hp))
    kpad = jnp.pad(kern.astype(BF16), ((0, 0), (0, 0), (0, 0), (GROUP_COLS, 0)))
    a_mat = jnp.stack([kpad[..., GROUP_COLS - SSM_GROUP * t:2 * GROUP_COLS - SSM_GROUP * t]
                       for t in range(T_CHUNK)], axis=2).reshape(depth, n_groups, GROUP_COLS, GROUP_COLS)
    rev_re = jnp.swapaxes(pw_re[..., T_CHUNK - 1::-1], 2, 3)[:, :, :, None, :]
    rev_im = jnp.swapaxes(pw_im[..., T_CHUNK - 1::-1], 2, 3)[:, :, :, None, :]
    bt_re, bt_im = jnp.swapaxes(bb_re, 2, 3)[:, :, None], jnp.swapaxes(bb_im, 2, 3)[:, :, None]
    pair = lambda m: m.reshape((depth, n_groups // 2, 2) + m.shape[2:])
    bp_re = pair((rev_re * bt_re - rev_im * bt_im).astype(BF16).reshape(depth, n_groups, GROUP_COLS, n_state))
    bp_im = pair((rev_re * bt_im + rev_im * bt_re).astype(BF16).reshape(depth, n_groups, GROUP_COLS, n_state))
    zeros = jnp.zeros_like(bp_re[:, :, 0])
    top = jnp.concatenate([bp_re[:, :, 0], zeros, bp_im[:, :, 0], zeros], axis=-1)
    bot = jnp.concatenate([zeros, bp_re[:, :, 1], zeros, bp_im[:, :, 1]], axis=-1)
    bp = jnp.concatenate([top, bot], axis=-2)
    co_re = pair(flat(cw_re[:, :, :, 1:]).astype(BF16))
    co_im = pair(flat(-cw_im[:, :, :, 1:]).astype(BF16))
    zc = jnp.zeros_like(co_re[:, :, 0])
    cp = jnp.concatenate([
        jnp.concatenate([co_re[:, :, 0], zc], axis=-1),
        jnp.concatenate([zc, co_re[:, :, 1]], axis=-1),
        jnp.concatenate([co_im[:, :, 0], zc], axis=-1),
        jnp.concatenate([zc, co_im[:, :, 1]], axis=-1)], axis=-2)
    a_re = pw_re[..., T_CHUNK].reshape(depth, 1, -1)
    a_im = pw_im[..., T_CHUNK].reshape(depth, 1, -1)
    return a_mat, bp, cp, a_re, a_im


def _s5(proj, mats, ssm_d, layer, ssm_width):
    a_mat, bp, cp, a_re, a_im = mats
    L = proj.shape[0]
    n_groups = ssm_width // SSM_GROUP
    n_tiles = ssm_width // LANES
    n_chunks = L // T_CHUNK
    bc = min(S5_CHUNK_ROWS, n_chunks)
    bm = bc * T_CHUNK
    pairs = GROUPS_PER_TILE // 2
    state_w = n_groups * SSM_STATE
    perm = jnp.asarray(_regroup_matrix(), BF16)
    permt = jnp.asarray(_regroup_matrix().T, BF16)
    n_perm = perm.shape[0]

    z, s_re, s_im = pl.pallas_call(
        functools.partial(_s5_in_kernel, bc=bc),
        grid=(n_tiles, n_chunks // bc),
        in_specs=[pl.BlockSpec((bm, LANES), lambda v, r: (r, v)),
                  pl.BlockSpec((n_perm, n_perm), lambda v, r: (0, 0)),
                  pl.BlockSpec((None, pairs, 2 * GROUP_COLS, 4 * SSM_STATE), lambda v, r: (layer, v, 0, 0))],
        out_specs=[pl.BlockSpec((bc, GROUPS_PER_TILE * GROUP_COLS), lambda v, r: (r, v)),
                   pl.BlockSpec((bc, pairs * LANES), lambda v, r: (r, v)),
                   pl.BlockSpec((bc, pairs * LANES), lambda v, r: (r, v))],
        out_shape=[jax.ShapeDtypeStruct((n_chunks, n_groups * GROUP_COLS), BF16),
                   jax.ShapeDtypeStruct((n_chunks, state_w), F32),
                   jax.ShapeDtypeStruct((n_chunks, state_w), F32)],
        compiler_params=_params("parallel", "parallel"),
        name="s5_in",
    )(proj, perm, bp)

    bw = min(SCAN_COLS, state_w)
    p_re, p_im = pl.pallas_call(
        _s5_scan_kernel,
        grid=(state_w // bw,),
        in_specs=[pl.BlockSpec((n_chunks, bw), lambda c: (0, c)),
                  pl.BlockSpec((n_chunks, bw), lambda c: (0, c)),
                  pl.BlockSpec((None, 1, bw), lambda c: (layer, 0, c)),
                  pl.BlockSpec((None, 1, bw), lambda c: (layer, 0, c))],
        out_specs=[pl.BlockSpec((n_chunks, bw), lambda c: (0, c)),
                   pl.BlockSpec((n_chunks, bw), lambda c: (0, c))],
        out_shape=[jax.ShapeDtypeStruct((n_chunks, state_w), F32),
                   jax.ShapeDtypeStruct((n_chunks, state_w), F32)],
        compiler_params=_params("parallel"),
        name="s5_scan",
    )(s_re, s_im, a_re, a_im)

    yf, yb = pl.pallas_call(
        functools.partial(_s5_out_kernel, bc=bc),
        grid=(n_tiles, n_chunks // bc),
        in_specs=[pl.BlockSpec((bc, GROUPS_PER_TILE * GROUP_COLS), lambda v, r: (r, v)),
                  pl.BlockSpec((bc, pairs * LANES), lambda v, r: (r, v)),
                  pl.BlockSpec((bc, pairs * LANES), lambda v, r: (r, v)),
                  pl.BlockSpec((None, GROUPS_PER_TILE, GROUP_COLS, GROUP_COLS), lambda v, r: (layer, v, 0, 0)),
                  pl.BlockSpec((None, pairs, 4 * SSM_STATE, 2 * GROUP_COLS), lambda v, r: (layer, v, 0, 0)),
                  pl.BlockSpec((n_perm, n_perm), lambda v, r: (0, 0)),
                  pl.BlockSpec((bm, LANES), lambda v, r: (r, v)),
                  pl.BlockSpec((None, 1, LANES), lambda v, r: (layer, 0, v))],
        out_specs=[pl.BlockSpec((bm, LANES), lambda v, r: (r, v)),
                   pl.BlockSpec((bm, LANES), lambda v, r: (r, v))],
        out_shape=[jax.ShapeDtypeStruct((L, ssm_width), F32),
                   jax.ShapeDtypeStruct((L, ssm_width), BF16)],
        scratch_shapes=[pltpu.VMEM((bm, LANES), F32)],
        compiler_params=_params("parallel", "parallel"),
        name="s5_out",
    )(z, p_re, p_im, a_mat, cp, permt, proj, ssm_d)
    return yf, yb


def _glu_kernel(yb_ref, w_ref, b_ref, yf_ref, o_ref):
    gate = _dot(yb_ref[...], w_ref[...]) + b_ref[...]
    o_ref[...] = (yf_ref[...] * _sigmoid(gate)).astype(BF16)


def _glu(yf, yb, w_glu_b, b_glu, layer):
    L, W = yf.shape
    bm, bn = min(ROW_BLOCK, L), min(EPI_COLS, W)
    return pl.pallas_call(
        _glu_kernel,
        grid=(L // bm, W // bn),
        in_specs=[pl.BlockSpec((bm, W), lambda i, j: (i, 0)),
                  pl.BlockSpec((None, W, bn), lambda i, j: (layer, 0, j)),
                  pl.BlockSpec((None, 1, bn), lambda i, j: (layer, 0, j)),
                  pl.BlockSpec((bm, bn), lambda i, j: (i, j))],
        out_specs=pl.BlockSpec((bm, bn), lambda i, j: (i, j)),
        out_shape=jax.ShapeDtypeStruct((L, W), BF16),
        compiler_params=_params("parallel", "parallel"),
        name="glu",
    )(yb, w_glu_b, b_glu, yf)


def _pool_kernel(x_ref, prev_ref, w_ref, sc_ref, o_ref, xe_ref, *, bm, gw):
    i = pl.program_id(0)

    @pl.when(i == 0)
    def _():
        xe_ref[0:POOL_HALO, :] = jnp.zeros((POOL_HALO, xe_ref.shape[1]), F32)

    @pl.when(i > 0)
    def _():
        xe_ref[0:POOL_HALO, :] = prev_ref[...]

    xe_ref[POOL_HALO:POOL_HALO + bm, :] = x_ref[...]
    pos = i * bm + lax.broadcasted_iota(jnp.int32, (bm, 1), 0)
    for gi, win in enumerate(POOL_WINDOWS):
        cols = slice(gw * gi, gw * (gi + 1))
        x = x_ref[:, cols]
        acc = x
        for k in range(1, win):
            acc = acc + xe_ref[POOL_HALO - k:POOL_HALO - k + bm, cols]
        inv_cnt = 1.0 / jnp.minimum(pos + 1, win).astype(F32)
        pooled = acc * inv_cnt - x
        mixed = _dot(pooled.astype(BF16), w_ref[gi]) * sc_ref[:, cols]
        o_ref[:, cols] = mixed.astype(BF16)


def _pool(proj, pool_w_b, pool_scale, layer, col0, width):
    L = proj.shape[0]
    gw = width // len(POOL_WINDOWS)
    bm = min(POOL_ROWS, L)
    assert col0 % width == 0 and bm % POOL_HALO == 0
    cb = col0 // width
    halo_blocks = bm // POOL_HALO
    return pl.pallas_call(
        functools.partial(_pool_kernel, bm=bm, gw=gw),
        grid=(L // bm,),
        in_specs=[pl.BlockSpec((bm, width), lambda i: (i, cb)),
                  pl.BlockSpec((POOL_HALO, width), lambda i: (jnp.maximum(i * halo_blocks - 1, 0), cb)),
                  pl.BlockSpec((None, len(POOL_WINDOWS), gw, gw), lambda i: (layer, 0, 0, 0)),
                  pl.BlockSpec((None, 1, width), lambda i: (layer, 0, 0))],
        out_specs=pl.BlockSpec((bm, width), lambda i: (i, 0)),
        out_shape=jax.ShapeDtypeStruct((L, width), BF16),
        scratch_shapes=[pltpu.VMEM((bm + POOL_HALO, width), F32)],
        compiler_params=_params("parallel"),
        name="pool",
    )(proj, proj, pool_w_b, pool_scale)


def _merge_kernel(a_ref, b_ref, wa_ref, wb_ref, ga_ref, gb_ref, o_ref):
    ya = _dot(a_ref[...], wa_ref[...])
    yb = _dot(b_ref[...], wb_ref[...])
    o_ref[...] = (_sigmoid(ga_ref[...]) * ya + _sigmoid(gb_ref[...]) * yb).astype(BF16)


def _merge(za, zb, w_a, w_b, proj, layer, gate_col0):
    L, W = za.shape
    D = w_a.shape[2]
    bm, bn = min(ROW_BLOCK, L), min(EPI_COLS, D)
    assert gate_col0 % bn == 0 and D % bn == 0
    ga0 = gate_col0 // bn
    gb0 = (gate_col0 + D) // bn
    return pl.pallas_call(
        _merge_kernel,
        grid=(L // bm, D // bn),
        in_specs=[pl.BlockSpec((bm, W), lambda i, j: (i, 0)),
                  pl.BlockSpec((bm, W), lambda i, j: (i, 0)),
                  pl.BlockSpec((None, W, bn), lambda i, j: (layer, 0, j)),
                  pl.BlockSpec((None, W, bn), lambda i, j: (layer, 0, j)),
                  pl.BlockSpec((bm, bn), lambda i, j: (i, ga0 + j)),
                  pl.BlockSpec((bm, bn), lambda i, j: (i, gb0 + j))],
        out_specs=pl.BlockSpec((bm, bn), lambda i, j: (i, j)),
        out_shape=jax.ShapeDtypeStruct((L, D), BF16),
        compiler_params=_params("parallel", "parallel"),
        name="merge",
    )(za, zb, w_a, w_b, proj, proj)


def _resid_kernel(x_ref, w_ref, h_ref, o_ref, *, alpha):
    o_ref[...] = alpha * h_ref[...] + _dot(x_ref[...], w_ref[...])


def _resid_matmul(xb, w_b, h, layer, alpha, bm, name):
    L, K = xb.shape
    D = w_b.shape[2]
    bm, bn = min(bm, L), min(EPI_COLS, D)
    return pl.pallas_call(
        functools.partial(_resid_kernel, alpha=alpha),
        grid=(L // bm, D // bn),
        in_specs=[pl.BlockSpec((bm, K), lambda i, j: (i, 0)),
                  pl.BlockSpec((None, K, bn), lambda i, j: (layer, 0, j)),
                  pl.BlockSpec((bm, bn), lambda i, j: (i, j))],
        out_specs=pl.BlockSpec((bm, bn), lambda i, j: (i, j)),
        out_shape=jax.ShapeDtypeStruct((L, D), F32),
        compiler_params=_params("parallel", "parallel"),
        name=name,
    )(xb, w_b, h)


def _resid_ln_kernel(x_ref, w_ref, h_ref, g_ref, b_ref, of_ref, ob_ref, pre_ref, *, alpha, bn):
    j = pl.program_id(1)
    n_col = pre_ref.shape[0]
    pre_ref[j] = alpha * h_ref[...] + _dot(x_ref[...], w_ref[...])

    @pl.when(j == n_col - 1)
    def _():
        inv_d = 1.0 / (n_col * bn)

        total = pre_ref[0].sum(axis=-1, keepdims=True)
        for k in range(1, n_col):
            total = total + pre_ref[k].sum(axis=-1, keepdims=True)
        mu = total * inv_d
        sq = jnp.square(pre_ref[0] - mu).sum(axis=-1, keepdims=True)
        for k in range(1, n_col):
            sq = sq + jnp.square(pre_ref[k] - mu).sum(axis=-1, keepdims=True)
        rstd = lax.rsqrt(sq * inv_d + LN_EPS)
        for k in range(n_col):
            cols = slice(bn * k, bn * (k + 1))
            y = (pre_ref[k] - mu) * rstd * g_ref[:, cols] + b_ref[:, cols]
            of_ref[:, cols] = y
            ob_ref[:, cols] = y.astype(BF16)


def _resid_matmul_ln(xb, w_b, h, g, b, layer, alpha, name):
    L, K = xb.shape
    D = w_b.shape[2]
    bm, bn = min(LN_FUSED_ROWS, L), min(EPI_COLS, D)
    return pl.pallas_call(
        functools.partial(_resid_ln_kernel, alpha=alpha, bn=bn),
        grid=(L // bm, D // bn),
        in_specs=[pl.BlockSpec((bm, K), lambda i, j: (i, 0)),
                  pl.BlockSpec((None, K, bn), lambda i, j: (layer, 0, j)),
                  pl.BlockSpec((bm, bn), lambda i, j: (i, j)),
                  pl.BlockSpec((None, 1, D), lambda i, j: (layer, 0, 0)),
                  pl.BlockSpec((None, 1, D), lambda i, j: (layer, 0, 0))],
        out_specs=[pl.BlockSpec((bm, D), lambda i, j: (i, 0)),
                   pl.BlockSpec((bm, D), lambda i, j: (i, 0))],
        out_shape=[jax.ShapeDtypeStruct((L, D), F32), jax.ShapeDtypeStruct((L, D), BF16)],
        scratch_shapes=[pltpu.VMEM((D // bn, bm, bn), F32)],
        compiler_params=_params("parallel", "arbitrary", vmem=LN_FUSED_VMEM_LIMIT),
        name=name,
    )(xb, w_b, h, g, b)


def _ln_kernel(x_ref, g_ref, b_ref, of_ref, ob_ref):
    x = x_ref[...]
    mu = jnp.mean(x, axis=-1, keepdims=True)
    xc = x - mu
    var = jnp.mean(xc * xc, axis=-1, keepdims=True)
    y = xc * lax.rsqrt(var + LN_EPS) * g_ref[...] + b_ref[...]
    of_ref[...] = y
    ob_ref[...] = y.astype(BF16)


def _layer_norm(x, g, b, layer):
    L, D = x.shape
    bm = min(LN_ROWS, L)
    return pl.pallas_call(
        _ln_kernel,
        grid=(L // bm,),
        in_specs=[pl.BlockSpec((bm, D), lambda i: (i, 0)),
                  pl.BlockSpec((None, 1, D), lambda i: (layer, 0, 0)),
                  pl.BlockSpec((None, 1, D), lambda i: (layer, 0, 0))],
        out_specs=[pl.BlockSpec((bm, D), lambda i: (i, 0)),
                   pl.BlockSpec((bm, D), lambda i: (i, 0))],
        out_shape=[jax.ShapeDtypeStruct((L, D), F32), jax.ShapeDtypeStruct((L, D), BF16)],
        compiler_params=_params("parallel"),
        name="layer_norm",
    )(x, g, b)


CONV_PAD = 8


def _up_kernel(x_ref, wg_ref, wv_ref, cw_ref, cb_ref, o_ref, gbuf_ref, *, bm):
    i = pl.program_id(1)
    x = x_ref[...]

    @pl.when(i == 0)
    def _():
        gbuf_ref[0:CONV_PAD, :] = jnp.zeros((CONV_PAD, gbuf_ref.shape[1]), F32)

    @pl.when(i > 0)
    def _():
        gbuf_ref[0:CONV_PAD, :] = gbuf_ref[bm:bm + CONV_PAD, :]

    for s in range(o_ref.shape[1] // MXU_COLS):
        cols = slice(MXU_COLS * s, MXU_COLS * (s + 1))
        gate = _dot(x, wg_ref[:, cols])
        val = _dot(x, wv_ref[:, cols])
        gbuf_ref[CONV_PAD:CONV_PAD + bm, cols] = gate
        conv = cw_ref[2:3, cols] * gate + cb_ref[:, cols]
        conv = conv + cw_ref[1:2, cols] * gbuf_ref[CONV_PAD - 1:CONV_PAD - 1 + bm, cols]
        conv = conv + cw_ref[0:1, cols] * gbuf_ref[CONV_PAD - 2:CONV_PAD - 2 + bm, cols]
        o_ref[:, cols] = (_gelu(conv) * val).astype(BF16)


def _up(hb, w_up_b, conv_w, conv_b, layer):
    L, D = hb.shape
    d_ff = w_up_b.shape[2] // 2
    bm, bn = min(UP_ROWS, L), min(UP_COLS, d_ff)
    nb = d_ff // bn
    return pl.pallas_call(
        functools.partial(_up_kernel, bm=bm),
        grid=(nb, L // bm),
        in_specs=[pl.BlockSpec((bm, D), lambda j, i: (i, 0)),
                  pl.BlockSpec((None, D, bn), lambda j, i: (layer, 0, j)),
                  pl.BlockSpec((None, D, bn), lambda j, i: (layer, 0, nb + j)),
                  pl.BlockSpec((None, CONV_WIDTH, bn), lambda j, i: (layer, 0, j)),
                  pl.BlockSpec((None, 1, bn), lambda j, i: (layer, 0, j))],
        out_specs=pl.BlockSpec((bm, bn), lambda j, i: (i, j)),
        out_shape=jax.ShapeDtypeStruct((L, d_ff), BF16),
        scratch_shapes=[pltpu.VMEM((bm + CONV_PAD, bn), F32)],
        compiler_params=_params("arbitrary", "arbitrary"),
        name="up",
    )(hb, w_up_b, w_up_b, conv_w, conv_b)


def kernel(x, w_in, ssm_lam_re, ssm_lam_im, ssm_log_dt, ssm_b_re, ssm_b_im, ssm_c_re, ssm_c_im, ssm_d, w_glu, b_glu, pool_w, pool_scale, w_br_ssm, w_br_pool, w_out, ln1_g, ln1_b, w_up, conv_w, conv_b, w_down, ln2_g, ln2_b):
    bsz, seq, d_model = x.shape
    depth = w_in.shape[0]
    ssm_width = w_glu.shape[1]
    pool_width = pool_scale.shape[1]
    alpha = (2 * depth) ** 0.25
    assert bsz == 1 and seq % (T_CHUNK * 8) == 0 and ssm_width % LANES == 0

    w_in_b, w_glu_b, pool_w_b = w_in.astype(BF16), w_glu.astype(BF16), pool_w.astype(BF16)
    w_br_ssm_b, w_br_pool_b, w_out_b = w_br_ssm.astype(BF16), w_br_pool.astype(BF16), w_out.astype(BF16)
    w_up_b, w_down_b = w_up.astype(BF16), w_down.astype(BF16)
    mats = _s5_matrices(ssm_lam_re, ssm_lam_im, ssm_log_dt, ssm_b_re, ssm_b_im, ssm_c_re, ssm_c_im)
    row = lambda a: a.reshape(depth, 1, -1)
    ssm_d_r, b_glu_r, pool_scale_r, conv_b_r = row(ssm_d), row(b_glu), row(pool_scale), row(conv_b)
    ln1_g_r, ln1_b_r, ln2_g_r, ln2_b_r = row(ln1_g), row(ln1_b), row(ln2_g), row(ln2_b)

    h = x.reshape(seq, d_model)
    hb = h.astype(BF16)
    for layer in range(depth):
        proj = _proj(hb, w_in_b, layer)
        yf, yb = _s5(proj, mats, ssm_d_r, layer, ssm_width)
        za = _glu(yf, yb, w_glu_b, b_glu_r, layer)
        zb = _pool(proj, pool_w_b, pool_scale_r, layer, ssm_width, pool_width)
        merged = _merge(za, zb, w_br_ssm_b, w_br_pool_b, proj, layer, ssm_width + pool_width)
        h, hb = _resid_matmul_ln(merged, w_out_b, h, ln1_g_r, ln1_b_r, layer, alpha, "wout")
        act = _up(hb, w_up_b, conv_w, conv_b_r, layer)
        pre = _resid_matmul(act, w_down_b, h, layer, alpha, DOWN_ROWS, "down")
        h, hb = _layer_norm(pre, ln2_g_r, ln2_b_r, layer)
    return h.reshape(bsz, seq, d_model)
```

```python
import functools
import math

import numpy as np
import jax
import jax.numpy as jnp
from jax import lax
from jax.experimental import pallas as pl
from jax.experimental.pallas import tpu as pltpu

F32 = jnp.float32
BF16 = jnp.bfloat16

SSM_GROUP = 16
SSM_STATE = 64
POOL_WINDOWS = (2, 4, 8, 16)
CONV_WIDTH = 3
LN_EPS = 1e-5

LANES = 128
T_CHUNK = 16
GROUPS_PER_TILE = LANES // SSM_GROUP
HALF_T = LANES // SSM_GROUP
GROUP_COLS = T_CHUNK * SSM_GROUP
POOL_HALO = 16
VMEM_LIMIT = 56 * 1024 * 1024
LN_FUSED_VMEM_LIMIT = 60 * 1024 * 1024

ROW_BLOCK = 1024
PROJ_COLS = 1024
EPI_COLS = 512
DOWN_ROWS = 1024
DOWN_COLS = 256
UP_ROWS = 512
UP_COLS = 1024
MXU_COLS = 256
S5_CHUNK_ROWS = 256
SCAN_COLS = 512
POOL_ROWS = 512
LN_ROWS = 256
LN_FUSED_ROWS = 512


def _gelu(x):
    c = math.sqrt(2.0 / math.pi)
    return x * (0.5 * (1.0 + jnp.tanh(c * (x + 0.044715 * (x * x * x)))))


def _sigmoid(x):
    return 0.5 * jnp.tanh(0.5 * x) + 0.5


def _params(*sem, vmem=VMEM_LIMIT):
    return pltpu.CompilerParams(dimension_semantics=sem, vmem_limit_bytes=vmem)


def _dot(a, b):
    return jnp.dot(a, b, preferred_element_type=F32)


def _proj_kernel(x_ref, w_ref, o_ref):
    o_ref[...] = _dot(x_ref[...], w_ref[...])


def _proj(hb, w_in_b, layer):
    L, D = hb.shape
    n_out = w_in_b.shape[2]
    bm, bn = min(ROW_BLOCK, L), min(PROJ_COLS, n_out)
    return pl.pallas_call(
        _proj_kernel,
        grid=(L // bm, n_out // bn),
        in_specs=[pl.BlockSpec((bm, D), lambda i, j: (i, 0)),
                  pl.BlockSpec((None, D, bn), lambda i, j: (layer, 0, j))],
        out_specs=pl.BlockSpec((bm, bn), lambda i, j: (i, j)),
        out_shape=jax.ShapeDtypeStruct((L, n_out), F32),
        compiler_params=_params("parallel", "parallel"),
        name="proj",
    )(hb, w_in_b)


def _regroup_matrix():
    n = HALF_T * LANES
    src = np.arange(n)
    t8, g8, h = src // LANES, (src % LANES) // SSM_GROUP, src % SSM_GROUP
    dst = g8 * LANES + t8 * SSM_GROUP + h
    m = np.zeros((n, n), np.float32)
    m[src, dst] = 1.0
    return m


def _s5_in_kernel(u_ref, perm_ref, bp_ref, z_ref, sre_ref, sim_ref, *, bc):
    halves = []
    for j in range(T_CHUNK // HALF_T):
        xcat = jnp.concatenate(
            [u_ref[pl.ds(HALF_T * j + t8, bc, stride=T_CHUNK), :].astype(BF16) for t8 in range(HALF_T)],
            axis=1)
        halves.append(_dot(xcat, perm_ref[...]).astype(BF16))
    for g8 in range(GROUPS_PER_TILE):
        for j, half in enumerate(halves):
            z_ref[:, GROUP_COLS * g8 + LANES * j:GROUP_COLS * g8 + LANES * (j + 1)] = (
                half[:, LANES * g8:LANES * (g8 + 1)])
    for q in range(GROUPS_PER_TILE // 2):
        s = _dot(z_ref[:, 2 * GROUP_COLS * q:2 * GROUP_COLS * (q + 1)], bp_ref[q])
        sre_ref[:, LANES * q:LANES * (q + 1)] = s[:, :LANES]
        sim_ref[:, LANES * q:LANES * (q + 1)] = s[:, LANES:]


def _s5_scan_kernel(sre_ref, sim_ref, are_ref, aim_ref, pre_ref, pim_ref):
    n_chunks, width = sre_ref.shape
    ar = are_ref[...]
    ai = aim_ref[...]

    def body(r, carry):
        sr, si = carry
        row = pl.ds(r, 1)
        pre_ref[row, :] = sr
        pim_ref[row, :] = si
        nr = ar * sr - ai * si + sre_ref[row, :]
        ni = ar * si + ai * sr + sim_ref[row, :]
        return nr, ni

    zero = jnp.zeros((1, width), F32)
    lax.fori_loop(0, n_chunks, body, (zero, zero))


def _s5_out_kernel(z_ref, pre_ref, pim_ref, a_ref, cp_ref, permt_ref, u_ref, d_ref,
                   yf_ref, yb_ref, ybuf_ref, *, bc):
    ys = []
    for q in range(GROUPS_PER_TILE // 2):
        state = jnp.concatenate([pre_ref[:, LANES * q:LANES * (q + 1)],
                                 pim_ref[:, LANES * q:LANES * (q + 1)]], axis=1).astype(BF16)
        carried = _dot(state, cp_ref[q])
        for e in range(2):
            g8 = 2 * q + e
            local = _dot(z_ref[:, GROUP_COLS * g8:GROUP_COLS * (g8 + 1)], a_ref[g8])
            ys.append(local + carried[:, GROUP_COLS * e:GROUP_COLS * (e + 1)])
    for j in range(T_CHUNK // HALF_T):
        ycat = jnp.concatenate([y[:, LANES * j:LANES * (j + 1)] for y in ys], axis=1)
        yt = _dot(ycat.astype(BF16), permt_ref[...])
        for t8 in range(HALF_T):
            ybuf_ref[pl.ds(HALF_T * j + t8, bc, stride=T_CHUNK), :] = yt[:, LANES * t8:LANES * (t8 + 1)]
    y = _gelu(ybuf_ref[...] + d_ref[...] * u_ref[...])
    yf_ref[...] = y
    yb_ref[...] = y.astype(BF16)


def _s5_matrices(lam_re, lam_im, log_dt, b_re, b_im, c_re, c_im):
    hp = lax.Precision.HIGHEST
    depth, n_groups, n_state = lam_re.shape
    dt = jnp.exp(log_dt)[..., None]
    zr, zi = lam_re * dt, lam_im * dt
    steps = jnp.arange(T_CHUNK + 1, dtype=F32)
    mag = jnp.exp(zr[..., None] * steps)
    pw_re = mag * jnp.cos(zi[..., None] * steps)
    pw_im = mag * jnp.sin(zi[..., None] * steps)
    e1_re = jnp.expm1(zr) * jnp.cos(zi) - 2.0 * jnp.square(jnp.sin(0.5 * zi))
    e1_im = jnp.exp(zr) * jnp.sin(zi)
    den = lam_re * lam_re + lam_im * lam_im
    f_re = (e1_re * lam_re + e1_im * lam_im) / den
    f_im = (e1_im * lam_re - e1_re * lam_im) / den
    bb_re = f_re[..., None] * b_re - f_im[..., None] * b_im
    bb_im = f_re[..., None] * b_im + f_im[..., None] * b_re
    ct_re, ct_im = jnp.swapaxes(c_re, 2, 3)[:, :, :, None, :], jnp.swapaxes(c_im, 2, 3)[:, :, :, None, :]
    cw_re = ct_re * pw_re[..., None] - ct_im * pw_im[..., None]
    cw_im = ct_re * pw_im[..., None] + ct_im * pw_re[..., None]
    flat = lambda m: m.reshape(depth, n_groups, n_state, GROUP_COLS)
    kern = (jnp.einsum('lgpi,lgpn->lgin', bb_re, flat(cw_re[:, :, :, :T_CHUNK]), precision=hp)
            - jnp.einsum('lgpi,lgpn->lgin', bb_im, flat(cw_im[:, :, :, :T_CHUNK]), precision=hp))
    kpad = jnp.pad(kern.astype(BF16), ((0, 0), (0, 0), (0, 0), (GROUP_COLS, 0)))
    a_mat = jnp.stack([kpad[..., GROUP_COLS - SSM_GROUP * t:2 * GROUP_COLS - SSM_GROUP * t]
                       for t in range(T_CHUNK)], axis=2).reshape(depth, n_groups, GROUP_COLS, GROUP_COLS)
    rev_re = jnp.swapaxes(pw_re[..., T_CHUNK - 1::-1], 2, 3)[:, :, :, None, :]
    rev_im = jnp.swapaxes(pw_im[..., T_CHUNK - 1::-1], 2, 3)[:, :, :, None, :]
    bt_re, bt_im = jnp.swapaxes(bb_re, 2, 3)[:, :, None], jnp.swapaxes(bb_im, 2, 3)[:, :, None]
    pair = lambda m: m.reshape((depth, n_groups // 2, 2) + m.shape[2:])
    bp_re = pair((rev_re * bt_re - rev_im * bt_im).astype(BF16).reshape(depth, n_groups, GROUP_COLS, n_state))
    bp_im = pair((rev_re * bt_im + rev_im * bt_re).astype(BF16).reshape(depth, n_groups, GROUP_COLS, n_state))
    zeros = jnp.zeros_like(bp_re[:, :, 0])
    top = jnp.concatenate([bp_re[:, :, 0], zeros, bp_im[:, :, 0], zeros], axis=-1)
    bot = jnp.concatenate([zeros, bp_re[:, :, 1], zeros, bp_im[:, :, 1]], axis=-1)
    bp = jnp.concatenate([top, bot], axis=-2)
    co_re = pair(flat(cw_re[:, :, :, 1:]).astype(BF16))
    co_im = pair(flat(-cw_im[:, :, :, 1:]).astype(BF16))
    zc = jnp.zeros_like(co_re[:, :, 0])
    cp = jnp.concatenate([
        jnp.concatenate([co_re[:, :, 0], zc], axis=-1),
        jnp.concatenate([zc, co_re[:, :, 1]], axis=-1),
        jnp.concatenate([co_im[:, :, 0], zc], axis=-1),
        jnp.concatenate([zc, co_im[:, :, 1]], axis=-1)], axis=-2)
    a_re = pw_re[..., T_CHUNK].reshape(depth, 1, -1)
    a_im = pw_im[..., T_CHUNK].reshape(depth, 1, -1)
    return a_mat, bp, cp, a_re, a_im


def _s5(proj, mats, ssm_d, layer, ssm_width):
    a_mat, bp, cp, a_re, a_im = mats
    L = proj.shape[0]
    n_groups = ssm_width // SSM_GROUP
    n_tiles = ssm_width // LANES
    n_chunks = L // T_CHUNK
    bc = min(S5_CHUNK_ROWS, n_chunks)
    bm = bc * T_CHUNK
    pairs = GROUPS_PER_TILE // 2
    state_w = n_groups * SSM_STATE
    perm = jnp.asarray(_regroup_matrix(), BF16)
    permt = jnp.asarray(_regroup_matrix().T, BF16)
    n_perm = perm.shape[0]

    z, s_re, s_im = pl.pallas_call(
        functools.partial(_s5_in_kernel, bc=bc),
        grid=(n_tiles, n_chunks // bc),
        in_specs=[pl.BlockSpec((bm, LANES), lambda v, r: (r, v)),
                  pl.BlockSpec((n_perm, n_perm), lambda v, r: (0, 0)),
                  pl.BlockSpec((None, pairs, 2 * GROUP_COLS, 4 * SSM_STATE), lambda v, r: (layer, v, 0, 0))],
        out_specs=[pl.BlockSpec((bc, GROUPS_PER_TILE * GROUP_COLS), lambda v, r: (r, v)),
                   pl.BlockSpec((bc, pairs * LANES), lambda v, r: (r, v)),
                   pl.BlockSpec((bc, pairs * LANES), lambda v, r: (r, v))],
        out_shape=[jax.ShapeDtypeStruct((n_chunks, n_groups * GROUP_COLS), BF16),
                   jax.ShapeDtypeStruct((n_chunks, state_w), F32),
                   jax.ShapeDtypeStruct((n_chunks, state_w), F32)],
        compiler_params=_params("parallel", "parallel"),
        name="s5_in",
    )(proj, perm, bp)

    bw = min(SCAN_COLS, state_w)
    p_re, p_im = pl.pallas_call(
        _s5_scan_kernel,
        grid=(state_w // bw,),
        in_specs=[pl.BlockSpec((n_chunks, bw), lambda c: (0, c)),
                  pl.BlockSpec((n_chunks, bw), lambda c: (0, c)),
                  pl.BlockSpec((None, 1, bw), lambda c: (layer, 0, c)),
                  pl.BlockSpec((None, 1, bw), lambda c: (layer, 0, c))],
        out_specs=[pl.BlockSpec((n_chunks, bw), lambda c: (0, c)),
                   pl.BlockSpec((n_chunks, bw), lambda c: (0, c))],
        out_shape=[jax.ShapeDtypeStruct((n_chunks, state_w), F32),
                   jax.ShapeDtypeStruct((n_chunks, state_w), F32)],
        compiler_params=_params("parallel"),
        name="s5_scan",
    )(s_re, s_im, a_re, a_im)

    yf, yb = pl.pallas_call(
        functools.partial(_s5_out_kernel, bc=bc),
        grid=(n_tiles, n_chunks // bc),
        in_specs=[pl.BlockSpec((bc, GROUPS_PER_TILE * GROUP_COLS), lambda v, r: (r, v)),
                  pl.BlockSpec((bc, pairs * LANES), lambda v, r: (r, v)),
                  pl.BlockSpec((bc, pairs * LANES), lambda v, r: (r, v)),
                  pl.BlockSpec((None, GROUPS_PER_TILE, GROUP_COLS, GROUP_COLS), lambda v, r: (layer, v, 0, 0)),
                  pl.BlockSpec((None, pairs, 4 * SSM_STATE, 2 * GROUP_COLS), lambda v, r: (layer, v, 0, 0)),
                  pl.BlockSpec((n_perm, n_perm), lambda v, r: (0, 0)),
                  pl.BlockSpec((bm, LANES), lambda v, r: (r, v)),
                  pl.BlockSpec((None, 1, LANES), lambda v, r: (layer, 0, v))],
        out_specs=[pl.BlockSpec((bm, LANES), lambda v, r: (r, v)),
                   pl.BlockSpec((bm, LANES), lambda v, r: (r, v))],
        out_shape=[jax.ShapeDtypeStruct((L, ssm_width), F32),
                   jax.ShapeDtypeStruct((L, ssm_width), BF16)],
        scratch_shapes=[pltpu.VMEM((bm, LANES), F32)],
        compiler_params=_params("parallel", "parallel"),
        name="s5_out",
    )(z, p_re, p_im, a_mat, cp, permt, proj, ssm_d)
    return yf, yb


def _glu_kernel(yb_ref, w_ref, b_ref, yf_ref, o_ref):
    gate = _dot(yb_ref[...], w_ref[...]) + b_ref[...]
    o_ref[...] = (yf_ref[...] * _sigmoid(gate)).astype(BF16)


def _glu(yf, yb, w_glu_b, b_glu, layer):
    L, W = yf.shape
    bm, bn = min(ROW_BLOCK, L), min(EPI_COLS, W)
    return pl.pallas_call(
        _glu_kernel,
        grid=(L // bm, W // bn),
        in_specs=[pl.BlockSpec((bm, W), lambda i, j: (i, 0)),
                  pl.BlockSpec((None, W, bn), lambda i, j: (layer, 0, j)),
                  pl.BlockSpec((None, 1, bn), lambda i, j: (layer, 0, j)),
                  pl.BlockSpec((bm, bn), lambda i, j: (i, j))],
        out_specs=pl.BlockSpec((bm, bn), lambda i, j: (i, j)),
        out_shape=jax.ShapeDtypeStruct((L, W), BF16),
        compiler_params=_params("parallel", "parallel"),
        name="glu",
    )(yb, w_glu_b, b_glu, yf)


def _pool_kernel(x_ref, prev_ref, w_ref, sc_ref, o_ref, xe_ref, *, bm, gw):
    i = pl.program_id(0)

    @pl.when(i == 0)
    def _():
        xe_ref[0:POOL_HALO, :] = jnp.zeros((POOL_HALO, xe_ref.shape[1]), F32)

    @pl.when(i > 0)
    def _():
        xe_ref[0:POOL_HALO, :] = prev_ref[...]

    xe_ref[POOL_HALO:POOL_HALO + bm, :] = x_ref[...]
    pos = i * bm + lax.broadcasted_iota(jnp.int32, (bm, 1), 0)
    for gi, win in enumerate(POOL_WINDOWS):
        cols = slice(gw * gi, gw * (gi + 1))
        x = x_ref[:, cols]
        run = xe_ref[:, cols]
        span = 1
        while span < win:
            run = run + pltpu.roll(run, span, axis=0)
            span *= 2
        acc = run[POOL_HALO:, :]
        inv_cnt = 1.0 / jnp.minimum(pos + 1, win).astype(F32)
        pooled = acc * inv_cnt - x
        mixed = _dot(pooled.astype(BF16), w_ref[gi]) * sc_ref[:, cols]
        o_ref[:, cols] = mixed.astype(BF16)


def _pool(proj, pool_w_b, pool_scale, layer, col0, width):
    L = proj.shape[0]
    gw = width // len(POOL_WINDOWS)
    bm = min(POOL_ROWS, L)
    assert col0 % width == 0 and bm % POOL_HALO == 0
    cb = col0 // width
    halo_blocks = bm // POOL_HALO
    return pl.pallas_call(
        functools.partial(_pool_kernel, bm=bm, gw=gw),
        grid=(L // bm,),
        in_specs=[pl.BlockSpec((bm, width), lambda i: (i, cb)),
                  pl.BlockSpec((POOL_HALO, width), lambda i: (jnp.maximum(i * halo_blocks - 1, 0), cb)),
                  pl.BlockSpec((None, len(POOL_WINDOWS), gw, gw), lambda i: (layer, 0, 0, 0)),
                  pl.BlockSpec((None, 1, width), lambda i: (layer, 0, 0))],
        out_specs=pl.BlockSpec((bm, width), lambda i: (i, 0)),
        out_shape=jax.ShapeDtypeStruct((L, width), BF16),
        scratch_shapes=[pltpu.VMEM((bm + POOL_HALO, width), F32)],
        compiler_params=_params("parallel"),
        name="pool",
    )(proj, proj, pool_w_b, pool_scale)


def _merge_kernel(a_ref, b_ref, wa_ref, wb_ref, ga_ref, gb_ref, o_ref):
    ya = _dot(a_ref[...], wa_ref[...])
    yb = _dot(b_ref[...], wb_ref[...])
    o_ref[...] = (_sigmoid(ga_ref[...]) * ya + _sigmoid(gb_ref[...]) * yb).astype(BF16)


def _merge(za, zb, w_a, w_b, proj, layer, gate_col0):
    L, W = za.shape
    D = w_a.shape[2]
    bm, bn = min(ROW_BLOCK, L), min(EPI_COLS, D)
    assert gate_col0 % bn == 0 and D % bn == 0
    ga0 = gate_col0 // bn
    gb0 = (gate_col0 + D) // bn
    return pl.pallas_call(
        _merge_kernel,
        grid=(L // bm, D // bn),
        in_specs=[pl.BlockSpec((bm, W), lambda i, j: (i, 0)),
                  pl.BlockSpec((bm, W), lambda i, j: (i, 0)),
                  pl.BlockSpec((None, W, bn), lambda i, j: (layer, 0, j)),
                  pl.BlockSpec((None, W, bn), lambda i, j: (layer, 0, j)),
                  pl.BlockSpec((bm, bn), lambda i, j: (i, ga0 + j)),
                  pl.BlockSpec((bm, bn), lambda i, j: (i, gb0 + j))],
        out_specs=pl.BlockSpec((bm, bn), lambda i, j: (i, j)),
        out_shape=jax.ShapeDtypeStruct((L, D), BF16),
        compiler_params=_params("parallel", "parallel"),
        name="merge",
    )(za, zb, w_a, w_b, proj, proj)


def _resid_kernel(x_ref, w_ref, h_ref, o_ref, *, alpha):
    o_ref[...] = alpha * h_ref[...] + _dot(x_ref[...], w_ref[...])


def _resid_matmul(xb, w_b, h, layer, alpha, bm, bn, name):
    L, K = xb.shape
    D = w_b.shape[2]
    bm, bn = min(bm, L), min(bn, D)
    return pl.pallas_call(
        functools.partial(_resid_kernel, alpha=alpha),
        grid=(L // bm, D // bn),
        in_specs=[pl.BlockSpec((bm, K), lambda i, j: (i, 0)),
                  pl.BlockSpec((None, K, bn), lambda i, j: (layer, 0, j)),
                  pl.BlockSpec((bm, bn), lambda i, j: (i, j))],
        out_specs=pl.BlockSpec((bm, bn), lambda i, j: (i, j)),
        out_shape=jax.ShapeDtypeStruct((L, D), F32),
        compiler_params=_params("parallel", "parallel"),
        name=name,
    )(xb, w_b, h)


def _resid_ln_kernel(x_ref, w_ref, h_ref, g_ref, b_ref, of_ref, ob_ref, pre_ref, *, alpha, bn):
    j = pl.program_id(1)
    n_col = pre_ref.shape[0]
    pre_ref[j] = alpha * h_ref[...] + _dot(x_ref[...], w_ref[...])

    @pl.when(j == n_col - 1)
    def _():
        inv_d = 1.0 / (n_col * bn)

        total = pre_ref[0].sum(axis=-1, keepdims=True)
        for k in range(1, n_col):
            total = total + pre_ref[k].sum(axis=-1, keepdims=True)
        mu = total * inv_d
        sq = jnp.square(pre_ref[0] - mu).sum(axis=-1, keepdims=True)
        for k in range(1, n_col):
            sq = sq + jnp.square(pre_ref[k] - mu).sum(axis=-1, keepdims=True)
        rstd = lax.rsqrt(sq * inv_d + LN_EPS)
        for k in range(n_col):
            cols = slice(bn * k, bn * (k + 1))
            y = (pre_ref[k] - mu) * rstd * g_ref[:, cols] + b_ref[:, cols]
            of_ref[:, cols] = y
            ob_ref[:, cols] = y.astype(BF16)


def _resid_matmul_ln(xb, w_b, h, g, b, layer, alpha, name):
    L, K = xb.shape
    D = w_b.shape[2]
    bm, bn = min(LN_FUSED_ROWS, L), min(EPI_COLS, D)
    return pl.pallas_call(
        functools.partial(_resid_ln_kernel, alpha=alpha, bn=bn),
        grid=(L // bm, D // bn),
        in_specs=[pl.BlockSpec((bm, K), lambda i, j: (i, 0)),
                  pl.BlockSpec((None, K, bn), lambda i, j: (layer, 0, j)),
                  pl.BlockSpec((bm, bn), lambda i, j: (i, j)),
                  pl.BlockSpec((None, 1, D), lambda i, j: (layer, 0, 0)),
                  pl.BlockSpec((None, 1, D), lambda i, j: (layer, 0, 0))],
        out_specs=[pl.BlockSpec((bm, D), lambda i, j: (i, 0)),
                   pl.BlockSpec((bm, D), lambda i, j: (i, 0))],
        out_shape=[jax.ShapeDtypeStruct((L, D), F32), jax.ShapeDtypeStruct((L, D), BF16)],
        scratch_shapes=[pltpu.VMEM((D // bn, bm, bn), F32)],
        compiler_params=_params("parallel", "arbitrary", vmem=LN_FUSED_VMEM_LIMIT),
        name=name,
    )(xb, w_b, h, g, b)


def _ln_kernel(x_ref, g_ref, b_ref, of_ref, ob_ref):
    x = x_ref[...]
    mu = jnp.mean(x, axis=-1, keepdims=True)
    xc = x - mu
    var = jnp.mean(xc * xc, axis=-1, keepdims=True)
    y = xc * lax.rsqrt(var + LN_EPS) * g_ref[...] + b_ref[...]
    of_ref[...] = y
    ob_ref[...] = y.astype(BF16)


def _layer_norm(x, g, b, layer):
    L, D = x.shape
    bm = min(LN_ROWS, L)
    return pl.pallas_call(
        _ln_kernel,
        grid=(L // bm,),
        in_specs=[pl.BlockSpec((bm, D), lambda i: (i, 0)),
                  pl.BlockSpec((None, 1, D), lambda i: (layer, 0, 0)),
                  pl.BlockSpec((None, 1, D), lambda i: (layer, 0, 0))],
        out_specs=[pl.BlockSpec((bm, D), lambda i: (i, 0)),
                   pl.BlockSpec((bm, D), lambda i: (i, 0))],
        out_shape=[jax.ShapeDtypeStruct((L, D), F32), jax.ShapeDtypeStruct((L, D), BF16)],
        compiler_params=_params("parallel"),
        name="layer_norm",
    )(x, g, b)


CONV_PAD = 8


def _up_kernel(x_ref, wg_ref, wv_ref, cw_ref, cb_ref, o_ref, gbuf_ref, *, bm):
    i = pl.program_id(1)
    x = x_ref[...]

    @pl.when(i == 0)
    def _():
        gbuf_ref[0:CONV_PAD, :] = jnp.zeros((CONV_PAD, gbuf_ref.shape[1]), F32)

    @pl.when(i > 0)
    def _():
        gbuf_ref[0:CONV_PAD, :] = gbuf_ref[bm:bm + CONV_PAD, :]

    for s in range(o_ref.shape[1] // MXU_COLS):
        cols = slice(MXU_COLS * s, MXU_COLS * (s + 1))
        gate = _dot(x, wg_ref[:, cols])
        val = _dot(x, wv_ref[:, cols])
        gbuf_ref[CONV_PAD:CONV_PAD + bm, cols] = gate
        conv = cw_ref[2:3, cols] * gate + cb_ref[:, cols]
        conv = conv + cw_ref[1:2, cols] * gbuf_ref[CONV_PAD - 1:CONV_PAD - 1 + bm, cols]
        conv = conv + cw_ref[0:1, cols] * gbuf_ref[CONV_PAD - 2:CONV_PAD - 2 + bm, cols]
        o_ref[:, cols] = (_gelu(conv) * val).astype(BF16)


def _up(hb, w_up_b, conv_w, conv_b, layer):
    L, D = hb.shape
    d_ff = w_up_b.shape[2] // 2
    bm, bn = min(UP_ROWS, L), min(UP_COLS, d_ff)
    nb = d_ff // bn
    return pl.pallas_call(
        functools.partial(_up_kernel, bm=bm),
        grid=(nb, L // bm),
        in_specs=[pl.BlockSpec((bm, D), lambda j, i: (i, 0)),
                  pl.BlockSpec((None, D, bn), lambda j, i: (layer, 0, j)),
                  pl.BlockSpec((None, D, bn), lambda j, i: (layer, 0, nb + j)),
                  pl.BlockSpec((None, CONV_WIDTH, bn), lambda j, i: (layer, 0, j)),
                  pl.BlockSpec((None, 1, bn), lambda j, i: (layer, 0, j))],
        out_specs=pl.BlockSpec((bm, bn), lambda j, i: (i, j)),
        out_shape=jax.ShapeDtypeStruct((L, d_ff), BF16),
        scratch_shapes=[pltpu.VMEM((bm + CONV_PAD, bn), F32)],
        compiler_params=_params("arbitrary", "arbitrary"),
        name="up",
    )(hb, w_up_b, w_up_b, conv_w, conv_b)


def kernel(x, w_in, ssm_lam_re, ssm_lam_im, ssm_log_dt, ssm_b_re, ssm_b_im, ssm_c_re, ssm_c_im, ssm_d, w_glu, b_glu, pool_w, pool_scale, w_br_ssm, w_br_pool, w_out, ln1_g, ln1_b, w_up, conv_w, conv_b, w_down, ln2_g, ln2_b):
    bsz, seq, d_model = x.shape
    depth = w_in.shape[0]
    ssm_width = w_glu.shape[1]
    pool_width = pool_scale.shape[1]
    alpha = (2 * depth) ** 0.25
    assert bsz == 1 and seq % (T_CHUNK * 8) == 0 and ssm_width % LANES == 0

    w_in_b, w_glu_b, pool_w_b = w_in.astype(BF16), w_glu.astype(BF16), pool_w.astype(BF16)
    w_br_ssm_b, w_br_pool_b, w_out_b = w_br_ssm.astype(BF16), w_br_pool.astype(BF16), w_out.astype(BF16)
    w_up_b, w_down_b = w_up.astype(BF16), w_down.astype(BF16)
    mats = _s5_matrices(ssm_lam_re, ssm_lam_im, ssm_log_dt, ssm_b_re, ssm_b_im, ssm_c_re, ssm_c_im)
    row = lambda a: a.reshape(depth, 1, -1)
    ssm_d_r, b_glu_r, pool_scale_r, conv_b_r = row(ssm_d), row(b_glu), row(pool_scale), row(conv_b)
    ln1_g_r, ln1_b_r, ln2_g_r, ln2_b_r = row(ln1_g), row(ln1_b), row(ln2_g), row(ln2_b)

    h = x.reshape(seq, d_model)
    hb = h.astype(BF16)
    for layer in range(depth):
        proj = _proj(hb, w_in_b, layer)
        yf, yb = _s5(proj, mats, ssm_d_r, layer, ssm_width)
        za = _glu(yf, yb, w_glu_b, b_glu_r, layer)
        zb = _pool(proj, pool_w_b, pool_scale_r, layer, ssm_width, pool_width)
        merged = _merge(za, zb, w_br_ssm_b, w_br_pool_b, proj, layer, ssm_width + pool_width)
        h, hb = _resid_matmul_ln(merged, w_out_b, h, ln1_g_r, ln1_b_r, layer, alpha, "wout")
        act = _up(hb, w_up_b, conv_w, conv_b_r, layer)
        pre = _resid_matmul(act, w_down_b, h, layer, alpha, DOWN_ROWS, DOWN_COLS, "down")
        h, hb = _layer_norm(pre, ln2_g_r, ln2_b_r, layer)
    return h.reshape(bsz, seq, d_model)
```

```python
import functools
import math

import numpy as np
import jax
import jax.numpy as jnp
from jax import lax
from jax.experimental import pallas as pl
from jax.experimental.pallas import tpu as pltpu

F32 = jnp.float32
BF16 = jnp.bfloat16

SSM_GROUP = 16
SSM_STATE = 64
POOL_WINDOWS = (2, 4, 8, 16)
CONV_WIDTH = 3
LN_EPS = 1e-5

LANES = 128
T_CHUNK = 16
GROUPS_PER_TILE = LANES // SSM_GROUP
HALF_T = LANES // SSM_GROUP
GROUP_COLS = T_CHUNK * SSM_GROUP
POOL_HALO = 16
VMEM_LIMIT = 56 * 1024 * 1024
LN_FUSED_VMEM_LIMIT = 60 * 1024 * 1024
PROJ_VMEM_LIMIT = 60 * 1024 * 1024

ROW_BLOCK = 1024
PROJ_COLS = 1024
EPI_COLS = 512
DOWN_ROWS = 1024
DOWN_COLS = 256
UP_ROWS = 512
UP_COLS = 1024
MXU_COLS = 256
S5_CHUNK_ROWS = 256
SCAN_COLS = 512
POOL_ROWS = 512
LN_ROWS = 256
LN_FUSED_ROWS = 512


def _gelu(x):
    c = math.sqrt(2.0 / math.pi)
    return x * (0.5 * (1.0 + jnp.tanh(c * (x + 0.044715 * (x * x * x)))))


def _sigmoid(x):
    return 0.5 * jnp.tanh(0.5 * x) + 0.5


def _params(*sem, vmem=VMEM_LIMIT):
    return pltpu.CompilerParams(dimension_semantics=sem, vmem_limit_bytes=vmem)


def _dot(a, b):
    return jnp.dot(a, b, preferred_element_type=F32)


def _cast_specs(w, layer, n_blocks, block_index):
    _, rows, cols = w.shape
    slab = rows // n_blocks
    assert slab * n_blocks == rows and slab % 16 == 0, (rows, n_blocks)
    src = pl.BlockSpec((None, slab, cols), lambda *g: (layer, block_index(*g), 0))
    dst = pl.BlockSpec((None, slab, cols), lambda *g: (0, block_index(*g), 0))
    return src, dst, jax.ShapeDtypeStruct((1, rows, cols), BF16)


def _proj_kernel(x_ref, w_ref, up_src, down_src, o_ref, up_dst, down_dst):
    o_ref[...] = _dot(x_ref[...], w_ref[...])
    up_dst[...] = up_src[...].astype(BF16)
    down_dst[...] = down_src[...].astype(BF16)


def _proj(hb, w_in_b, w_in_layer, w_up, w_down, layer):
    L, D = hb.shape
    n_out = w_in_b.shape[2]
    bm, bn = min(ROW_BLOCK, L), min(PROJ_COLS, n_out)
    gi, gj = L // bm, n_out // bn
    per_row = 1 << (gj.bit_length() - 1)
    block_index = lambda i, j: i * per_row + jnp.minimum(j, per_row - 1)
    up_src, up_dst, up_shape = _cast_specs(w_up, layer, gi * per_row, block_index)
    down_src, down_dst, down_shape = _cast_specs(w_down, layer, gi * per_row, block_index)
    return pl.pallas_call(
        _proj_kernel,
        grid=(gi, gj),
        in_specs=[pl.BlockSpec((bm, D), lambda i, j: (i, 0)),
                  pl.BlockSpec((None, D, bn), lambda i, j: (w_in_layer, 0, j)),
                  up_src, down_src],
        out_specs=[pl.BlockSpec((bm, bn), lambda i, j: (i, j)), up_dst, down_dst],
        out_shape=[jax.ShapeDtypeStruct((L, n_out), F32), up_shape, down_shape],
        compiler_params=_params("arbitrary", "arbitrary", vmem=PROJ_VMEM_LIMIT),
        name="proj",
    )(hb, w_in_b, w_up, w_down)


def _regroup_matrix():
    n = HALF_T * LANES
    src = np.arange(n)
    t8, g8, h = src // LANES, (src % LANES) // SSM_GROUP, src % SSM_GROUP
    dst = g8 * LANES + t8 * SSM_GROUP + h
    m = np.zeros((n, n), np.float32)
    m[src, dst] = 1.0
    return m


def _s5_in_kernel(u_ref, perm_ref, bp_ref, z_ref, sre_ref, sim_ref, *, bc):
    halves = []
    for j in range(T_CHUNK // HALF_T):
        xcat = jnp.concatenate(
            [u_ref[pl.ds(HALF_T * j + t8, bc, stride=T_CHUNK), :].astype(BF16) for t8 in range(HALF_T)],
            axis=1)
        halves.append(_dot(xcat, perm_ref[...]).astype(BF16))
    for g8 in range(GROUPS_PER_TILE):
        for j, half in enumerate(halves):
            z_ref[:, GROUP_COLS * g8 + LANES * j:GROUP_COLS * g8 + LANES * (j + 1)] = (
                half[:, LANES * g8:LANES * (g8 + 1)])
    for q in range(GROUPS_PER_TILE // 2):
        s = _dot(z_ref[:, 2 * GROUP_COLS * q:2 * GROUP_COLS * (q + 1)], bp_ref[q])
        sre_ref[:, LANES * q:LANES * (q + 1)] = s[:, :LANES]
        sim_ref[:, LANES * q:LANES * (q + 1)] = s[:, LANES:]


def _s5_scan_kernel(sre_ref, sim_ref, are_ref, aim_ref, pre_ref, pim_ref):
    n_chunks, width = sre_ref.shape
    ar = are_ref[...]
    ai = aim_ref[...]

    def body(r, carry):
        sr, si = carry
        row = pl.ds(r, 1)
        pre_ref[row, :] = sr
        pim_ref[row, :] = si
        nr = ar * sr - ai * si + sre_ref[row, :]
        ni = ar * si + ai * sr + sim_ref[row, :]
        return nr, ni

    zero = jnp.zeros((1, width), F32)
    lax.fori_loop(0, n_chunks, body, (zero, zero))


def _s5_out_kernel(z_ref, pre_ref, pim_ref, a_ref, cp_ref, permt_ref, u_ref, d_ref,
                   yf_ref, yb_ref, ybuf_ref, *, bc):
    ys = []
    for q in range(GROUPS_PER_TILE // 2):
        state = jnp.concatenate([pre_ref[:, LANES * q:LANES * (q + 1)],
                                 pim_ref[:, LANES * q:LANES * (q + 1)]], axis=1).astype(BF16)
        carried = _dot(state, cp_ref[q])
        for e in range(2):
            g8 = 2 * q + e
            local = _dot(z_ref[:, GROUP_COLS * g8:GROUP_COLS * (g8 + 1)], a_ref[g8])
            ys.append(local + carried[:, GROUP_COLS * e:GROUP_COLS * (e + 1)])
    for j in range(T_CHUNK // HALF_T):
        ycat = jnp.concatenate([y[:, LANES * j:LANES * (j + 1)] for y in ys], axis=1)
        yt = _dot(ycat.astype(BF16), permt_ref[...])
        for t8 in range(HALF_T):
            ybuf_ref[pl.ds(HALF_T * j + t8, bc, stride=T_CHUNK), :] = yt[:, LANES * t8:LANES * (t8 + 1)]
    y = _gelu(ybuf_ref[...] + d_ref[...] * u_ref[...])
    yf_ref[...] = y
    yb_ref[...] = y.astype(BF16)


def _s5_matrices(lam_re, lam_im, log_dt, b_re, b_im, c_re, c_im):
    hp = lax.Precision.HIGHEST
    depth, n_groups, n_state = lam_re.shape
    dt = jnp.exp(log_dt)[..., None]
    zr, zi = lam_re * dt, lam_im * dt
    steps = jnp.arange(T_CHUNK + 1, dtype=F32)
    mag = jnp.exp(zr[..., None] * steps)
    pw_re = mag * jnp.cos(zi[..., None] * steps)
    pw_im = mag * jnp.sin(zi[..., None] * steps)
    e1_re = jnp.expm1(zr) * jnp.cos(zi) - 2.0 * jnp.square(jnp.sin(0.5 * zi))
    e1_im = jnp.exp(zr) * jnp.sin(zi)
    den = lam_re * lam_re + lam_im * lam_im
    f_re = (e1_re * lam_re + e1_im * lam_im) / den
    f_im = (e1_im * lam_re - e1_re * lam_im) / den
    bb_re = f_re[..., None] * b_re - f_im[..., None] * b_im
    bb_im = f_re[..., None] * b_im + f_im[..., None] * b_re
    ct_re, ct_im = jnp.swapaxes(c_re, 2, 3)[:, :, :, None, :], jnp.swapaxes(c_im, 2, 3)[:, :, :, None, :]
    cw_re = ct_re * pw_re[..., None] - ct_im * pw_im[..., None]
    cw_im = ct_re * pw_im[..., None] + ct_im * pw_re[..., None]
    flat = lambda m: m.reshape(depth, n_groups, n_state, GROUP_COLS)
    kern = (jnp.einsum('lgpi,lgpn->lgin', bb_re, flat(cw_re[:, :, :, :T_CHUNK]), precision=hp)
            - jnp.einsum('lgpi,lgpn->lgin', bb_im, flat(cw_im[:, :, :, :T_CHUNK]), precision=hp))
    kpad = jnp.pad(kern.astype(BF16), ((0, 0), (0, 0), (0, 0), (GROUP_COLS, 0)))
    a_mat = jnp.stack([kpad[..., GROUP_COLS - SSM_GROUP * t:2 * GROUP_COLS - SSM_GROUP * t]
                       for t in range(T_CHUNK)], axis=2).reshape(depth, n_groups, GROUP_COLS, GROUP_COLS)
    rev_re = jnp.swapaxes(pw_re[..., T_CHUNK - 1::-1], 2, 3)[:, :, :, None, :]
    rev_im = jnp.swapaxes(pw_im[..., T_CHUNK - 1::-1], 2, 3)[:, :, :, None, :]
    bt_re, bt_im = jnp.swapaxes(bb_re, 2, 3)[:, :, None], jnp.swapaxes(bb_im, 2, 3)[:, :, None]
    pair = lambda m: m.reshape((depth, n_groups // 2, 2) + m.shape[2:])
    bp_re = pair((rev_re * bt_re - rev_im * bt_im).astype(BF16).reshape(depth, n_groups, GROUP_COLS, n_state))
    bp_im = pair((rev_re * bt_im + rev_im * bt_re).astype(BF16).reshape(depth, n_groups, GROUP_COLS, n_state))
    zeros = jnp.zeros_like(bp_re[:, :, 0])
    top = jnp.concatenate([bp_re[:, :, 0], zeros, bp_im[:, :, 0], zeros], axis=-1)
    bot = jnp.concatenate([zeros, bp_re[:, :, 1], zeros, bp_im[:, :, 1]], axis=-1)
    bp = jnp.concatenate([top, bot], axis=-2)
    co_re = pair(flat(cw_re[:, :, :, 1:]).astype(BF16))
    co_im = pair(flat(-cw_im[:, :, :, 1:]).astype(BF16))
    zc = jnp.zeros_like(co_re[:, :, 0])
    cp = jnp.concatenate([
        jnp.concatenate([co_re[:, :, 0], zc], axis=-1),
        jnp.concatenate([zc, co_re[:, :, 1]], axis=-1),
        jnp.concatenate([co_im[:, :, 0], zc], axis=-1),
        jnp.concatenate([zc, co_im[:, :, 1]], axis=-1)], axis=-2)
    a_re = pw_re[..., T_CHUNK].reshape(depth, 1, -1)
    a_im = pw_im[..., T_CHUNK].reshape(depth, 1, -1)
    return a_mat, bp, cp, a_re, a_im


def _s5(proj, mats, ssm_d, layer, ssm_width):
    a_mat, bp, cp, a_re, a_im = mats
    L = proj.shape[0]
    n_groups = ssm_width // SSM_GROUP
    n_tiles = ssm_width // LANES
    n_chunks = L // T_CHUNK
    bc = min(S5_CHUNK_ROWS, n_chunks)
    bm = bc * T_CHUNK
    pairs = GROUPS_PER_TILE // 2
    state_w = n_groups * SSM_STATE
    perm = jnp.asarray(_regroup_matrix(), BF16)
    permt = jnp.asarray(_regroup_matrix().T, BF16)
    n_perm = perm.shape[0]

    z, s_re, s_im = pl.pallas_call(
        functools.partial(_s5_in_kernel, bc=bc),
        grid=(n_tiles, n_chunks // bc),
        in_specs=[pl.BlockSpec((bm, LANES), lambda v, r: (r, v)),
                  pl.BlockSpec((n_perm, n_perm), lambda v, r: (0, 0)),
                  pl.BlockSpec((None, pairs, 2 * GROUP_COLS, 4 * SSM_STATE), lambda v, r: (layer, v, 0, 0))],
        out_specs=[pl.BlockSpec((bc, GROUPS_PER_TILE * GROUP_COLS), lambda v, r: (r, v)),
                   pl.BlockSpec((bc, pairs * LANES), lambda v, r: (r, v)),
                   pl.BlockSpec((bc, pairs * LANES), lambda v, r: (r, v))],
        out_shape=[jax.ShapeDtypeStruct((n_chunks, n_groups * GROUP_COLS), BF16),
                   jax.ShapeDtypeStruct((n_chunks, state_w), F32),
                   jax.ShapeDtypeStruct((n_chunks, state_w), F32)],
        compiler_params=_params("parallel", "parallel"),
        name="s5_in",
    )(proj, perm, bp)

    bw = min(SCAN_COLS, state_w)
    p_re, p_im = pl.pallas_call(
        _s5_scan_kernel,
        grid=(state_w // bw,),
        in_specs=[pl.BlockSpec((n_chunks, bw), lambda c: (0, c)),
                  pl.BlockSpec((n_chunks, bw), lambda c: (0, c)),
                  pl.BlockSpec((None, 1, bw), lambda c: (layer, 0, c)),
                  pl.BlockSpec((None, 1, bw), lambda c: (layer, 0, c))],
        out_specs=[pl.BlockSpec((n_chunks, bw), lambda c: (0, c)),
                   pl.BlockSpec((n_chunks, bw), lambda c: (0, c))],
        out_shape=[jax.ShapeDtypeStruct((n_chunks, state_w), F32),
                   jax.ShapeDtypeStruct((n_chunks, state_w), F32)],
        compiler_params=_params("parallel"),
        name="s5_scan",
    )(s_re, s_im, a_re, a_im)

    yf, yb = pl.pallas_call(
        functools.partial(_s5_out_kernel, bc=bc),
        grid=(n_tiles, n_chunks // bc),
        in_specs=[pl.BlockSpec((bc, GROUPS_PER_TILE * GROUP_COLS), lambda v, r: (r, v)),
                  pl.BlockSpec((bc, pairs * LANES), lambda v, r: (r, v)),
                  pl.BlockSpec((bc, pairs * LANES), lambda v, r: (r, v)),
                  pl.BlockSpec((None, GROUPS_PER_TILE, GROUP_COLS, GROUP_COLS), lambda v, r: (layer, v, 0, 0)),
                  pl.BlockSpec((None, pairs, 4 * SSM_STATE, 2 * GROUP_COLS), lambda v, r: (layer, v, 0, 0)),
                  pl.BlockSpec((n_perm, n_perm), lambda v, r: (0, 0)),
                  pl.BlockSpec((bm, LANES), lambda v, r: (r, v)),
                  pl.BlockSpec((None, 1, LANES), lambda v, r: (layer, 0, v))],
        out_specs=[pl.BlockSpec((bm, LANES), lambda v, r: (r, v)),
                   pl.BlockSpec((bm, LANES), lambda v, r: (r, v))],
        out_shape=[jax.ShapeDtypeStruct((L, ssm_width), F32),
                   jax.ShapeDtypeStruct((L, ssm_width), BF16)],
        scratch_shapes=[pltpu.VMEM((bm, LANES), F32)],
        compiler_params=_params("parallel", "parallel"),
        name="s5_out",
    )(z, p_re, p_im, a_mat, cp, permt, proj, ssm_d)
    return yf, yb


def _glu_kernel(yb_ref, w_ref, b_ref, yf_ref, o_ref):
    gate = _dot(yb_ref[...], w_ref[...]) + b_ref[...]
    o_ref[...] = (yf_ref[...] * _sigmoid(gate)).astype(BF16)


def _glu(yf, yb, w_glu_b, b_glu, layer):
    L, W = yf.shape
    bm, bn = min(ROW_BLOCK, L), min(EPI_COLS, W)
    return pl.pallas_call(
        _glu_kernel,
        grid=(L // bm, W // bn),
        in_specs=[pl.BlockSpec((bm, W), lambda i, j: (i, 0)),
                  pl.BlockSpec((None, W, bn), lambda i, j: (layer, 0, j)),
                  pl.BlockSpec((None, 1, bn), lambda i, j: (layer, 0, j)),
                  pl.BlockSpec((bm, bn), lambda i, j: (i, j))],
        out_specs=pl.BlockSpec((bm, bn), lambda i, j: (i, j)),
        out_shape=jax.ShapeDtypeStruct((L, W), BF16),
        compiler_params=_params("parallel", "parallel"),
        name="glu",
    )(yb, w_glu_b, b_glu, yf)


def _pool_kernel(x_ref, prev_ref, w_ref, sc_ref, o_ref, xe_ref, *, bm, gw):
    i = pl.program_id(0)

    @pl.when(i == 0)
    def _():
        xe_ref[0:POOL_HALO, :] = jnp.zeros((POOL_HALO, xe_ref.shape[1]), F32)

    @pl.when(i > 0)
    def _():
        xe_ref[0:POOL_HALO, :] = prev_ref[...]

    xe_ref[POOL_HALO:POOL_HALO + bm, :] = x_ref[...]
    pos = i * bm + lax.broadcasted_iota(jnp.int32, (bm, 1), 0)
    for gi, win in enumerate(POOL_WINDOWS):
        cols = slice(gw * gi, gw * (gi + 1))
        x = x_ref[:, cols]
        run = xe_ref[:, cols]
        span = 1
        while span < win:
            run = run + pltpu.roll(run, span, axis=0)
            span *= 2
        acc = run[POOL_HALO:, :]
        inv_cnt = 1.0 / jnp.minimum(pos + 1, win).astype(F32)
        pooled = acc * inv_cnt - x
        mixed = _dot(pooled.astype(BF16), w_ref[gi]) * sc_ref[:, cols]
        o_ref[:, cols] = mixed.astype(BF16)


def _pool(proj, pool_w_b, pool_scale, layer, col0, width):
    L = proj.shape[0]
    gw = width // len(POOL_WINDOWS)
    bm = min(POOL_ROWS, L)
    assert col0 % width == 0 and bm % POOL_HALO == 0
    cb = col0 // width
    halo_blocks = bm // POOL_HALO
    return pl.pallas_call(
        functools.partial(_pool_kernel, bm=bm, gw=gw),
        grid=(L // bm,),
        in_specs=[pl.BlockSpec((bm, width), lambda i: (i, cb)),
                  pl.BlockSpec((POOL_HALO, width), lambda i: (jnp.maximum(i * halo_blocks - 1, 0), cb)),
                  pl.BlockSpec((None, len(POOL_WINDOWS), gw, gw), lambda i: (layer, 0, 0, 0)),
                  pl.BlockSpec((None, 1, width), lambda i: (layer, 0, 0))],
        out_specs=pl.BlockSpec((bm, width), lambda i: (i, 0)),
        out_shape=jax.ShapeDtypeStruct((L, width), BF16),
        scratch_shapes=[pltpu.VMEM((bm + POOL_HALO, width), F32)],
        compiler_params=_params("parallel"),
        name="pool",
    )(proj, proj, pool_w_b, pool_scale)


def _merge_kernel(a_ref, b_ref, wa_ref, wb_ref, ga_ref, gb_ref, o_ref):
    ya = _dot(a_ref[...], wa_ref[...])
    yb = _dot(b_ref[...], wb_ref[...])
    o_ref[...] = (_sigmoid(ga_ref[...]) * ya + _sigmoid(gb_ref[...]) * yb).astype(BF16)


def _merge(za, zb, w_a, w_b, proj, layer, gate_col0):
    L, W = za.shape
    D = w_a.shape[2]
    bm, bn = min(ROW_BLOCK, L), min(EPI_COLS, D)
    assert gate_col0 % bn == 0 and D % bn == 0
    ga0 = gate_col0 // bn
    gb0 = (gate_col0 + D) // bn
    return pl.pallas_call(
        _merge_kernel,
        grid=(L // bm, D // bn),
        in_specs=[pl.BlockSpec((bm, W), lambda i, j: (i, 0)),
                  pl.BlockSpec((bm, W), lambda i, j: (i, 0)),
                  pl.BlockSpec((None, W, bn), lambda i, j: (layer, 0, j)),
                  pl.BlockSpec((None, W, bn), lambda i, j: (layer, 0, j)),
                  pl.BlockSpec((bm, bn), lambda i, j: (i, ga0 + j)),
                  pl.BlockSpec((bm, bn), lambda i, j: (i, gb0 + j))],
        out_specs=pl.BlockSpec((bm, bn), lambda i, j: (i, j)),
        out_shape=jax.ShapeDtypeStruct((L, D), BF16),
        compiler_params=_params("parallel", "parallel"),
        name="merge",
    )(za, zb, w_a, w_b, proj, proj)


def _resid_kernel(x_ref, w_ref, h_ref, o_ref, *, alpha):
    o_ref[...] = alpha * h_ref[...] + _dot(x_ref[...], w_ref[...])


def _resid_matmul(xb, w_b, h, layer, alpha, bm, bn, name):
    L, K = xb.shape
    D = w_b.shape[2]
    bm, bn = min(bm, L), min(bn, D)
    return pl.pallas_call(
        functools.partial(_resid_kernel, alpha=alpha),
        grid=(L // bm, D // bn),
        in_specs=[pl.BlockSpec((bm, K), lambda i, j: (i, 0)),
                  pl.BlockSpec((None, K, bn), lambda i, j: (layer, 0, j)),
                  pl.BlockSpec((bm, bn), lambda i, j: (i, j))],
        out_specs=pl.BlockSpec((bm, bn), lambda i, j: (i, j)),
        out_shape=jax.ShapeDtypeStruct((L, D), F32),
        compiler_params=_params("parallel", "parallel"),
        name=name,
    )(xb, w_b, h)


def _resid_ln_kernel(x_ref, w_ref, h_ref, g_ref, b_ref, of_ref, ob_ref, pre_ref, *, alpha, bn):
    j = pl.program_id(1)
    n_col = pre_ref.shape[0]
    pre_ref[j] = alpha * h_ref[...] + _dot(x_ref[...], w_ref[...])

    @pl.when(j == n_col - 1)
    def _():
        inv_d = 1.0 / (n_col * bn)

        total = pre_ref[0].sum(axis=-1, keepdims=True)
        for k in range(1, n_col):
            total = total + pre_ref[k].sum(axis=-1, keepdims=True)
        mu = total * inv_d
        sq = jnp.square(pre_ref[0] - mu).sum(axis=-1, keepdims=True)
        for k in range(1, n_col):
            sq = sq + jnp.square(pre_ref[k] - mu).sum(axis=-1, keepdims=True)
        rstd = lax.rsqrt(sq * inv_d + LN_EPS)
        for k in range(n_col):
            cols = slice(bn * k, bn * (k + 1))
            y = (pre_ref[k] - mu) * rstd * g_ref[:, cols] + b_ref[:, cols]
            of_ref[:, cols] = y
            ob_ref[:, cols] = y.astype(BF16)


def _resid_matmul_ln(xb, w_b, h, g, b, layer, alpha, name):
    L, K = xb.shape
    D = w_b.shape[2]
    bm, bn = min(LN_FUSED_ROWS, L), min(EPI_COLS, D)
    return pl.pallas_call(
        functools.partial(_resid_ln_kernel, alpha=alpha, bn=bn),
        grid=(L // bm, D // bn),
        in_specs=[pl.BlockSpec((bm, K), lambda i, j: (i, 0)),
                  pl.BlockSpec((None, K, bn), lambda i, j: (layer, 0, j)),
                  pl.BlockSpec((bm, bn), lambda i, j: (i, j)),
                  pl.BlockSpec((None, 1, D), lambda i, j: (layer, 0, 0)),
                  pl.BlockSpec((None, 1, D), lambda i, j: (layer, 0, 0))],
        out_specs=[pl.BlockSpec((bm, D), lambda i, j: (i, 0)),
                   pl.BlockSpec((bm, D), lambda i, j: (i, 0))],
        out_shape=[jax.ShapeDtypeStruct((L, D), F32), jax.ShapeDtypeStruct((L, D), BF16)],
        scratch_shapes=[pltpu.VMEM((D // bn, bm, bn), F32)],
        compiler_params=_params("parallel", "arbitrary", vmem=LN_FUSED_VMEM_LIMIT),
        name=name,
    )(xb, w_b, h, g, b)


def _ln_kernel(x_ref, g_ref, b_ref, of_ref, ob_ref):
    x = x_ref[...]
    mu = jnp.mean(x, axis=-1, keepdims=True)
    xc = x - mu
    var = jnp.mean(xc * xc, axis=-1, keepdims=True)
    y = xc * lax.rsqrt(var + LN_EPS) * g_ref[...] + b_ref[...]
    of_ref[...] = y
    ob_ref[...] = y.astype(BF16)


def _layer_norm(x, g, b, layer):
    L, D = x.shape
    bm = min(LN_ROWS, L)
    return pl.pallas_call(
        _ln_kernel,
        grid=(L // bm,),
        in_specs=[pl.BlockSpec((bm, D), lambda i: (i, 0)),
                  pl.BlockSpec((None, 1, D), lambda i: (layer, 0, 0)),
                  pl.BlockSpec((None, 1, D), lambda i: (layer, 0, 0))],
        out_specs=[pl.BlockSpec((bm, D), lambda i: (i, 0)),
                   pl.BlockSpec((bm, D), lambda i: (i, 0))],
        out_shape=[jax.ShapeDtypeStruct((L, D), F32), jax.ShapeDtypeStruct((L, D), BF16)],
        compiler_params=_params("parallel"),
        name="layer_norm",
    )(x, g, b)


CONV_PAD = 8


def _up_kernel(x_ref, wg_ref, wv_ref, cw_ref, cb_ref, *rest, bm):
    if len(rest) == 4:
        next_src, o_ref, next_dst, gbuf_ref = rest
    else:
        (o_ref, gbuf_ref), next_src = rest, None
    i = pl.program_id(1)
    x = x_ref[...]

    @pl.when(i == 0)
    def _():
        gbuf_ref[0:CONV_PAD, :] = jnp.zeros((CONV_PAD, gbuf_ref.shape[1]), F32)

    @pl.when(i > 0)
    def _():
        gbuf_ref[0:CONV_PAD, :] = gbuf_ref[bm:bm + CONV_PAD, :]

    for s in range(o_ref.shape[1] // MXU_COLS):
        cols = slice(MXU_COLS * s, MXU_COLS * (s + 1))
        gate = _dot(x, wg_ref[:, cols])
        val = _dot(x, wv_ref[:, cols])
        gbuf_ref[CONV_PAD:CONV_PAD + bm, cols] = gate
        conv = cw_ref[2:3, cols] * gate + cb_ref[:, cols]
        conv = conv + cw_ref[1:2, cols] * gbuf_ref[CONV_PAD - 1:CONV_PAD - 1 + bm, cols]
        conv = conv + cw_ref[0:1, cols] * gbuf_ref[CONV_PAD - 2:CONV_PAD - 2 + bm, cols]
        o_ref[:, cols] = (_gelu(conv) * val).astype(BF16)
    if next_src is not None:
        next_dst[...] = next_src[...].astype(BF16)


def _up(hb, w_up_b, conv_w, conv_b, layer, w_in=None, next_layer=None):
    L, D = hb.shape
    d_ff = w_up_b.shape[2] // 2
    bm, bn = min(UP_ROWS, L), min(UP_COLS, d_ff)
    nb, ni = d_ff // bn, L // bm
    in_specs = [pl.BlockSpec((bm, D), lambda j, i: (i, 0)),
                pl.BlockSpec((None, D, bn), lambda j, i: (0, 0, j)),
                pl.BlockSpec((None, D, bn), lambda j, i: (0, 0, nb + j)),
                pl.BlockSpec((None, CONV_WIDTH, bn), lambda j, i: (layer, 0, j)),
                pl.BlockSpec((None, 1, bn), lambda j, i: (layer, 0, j))]
    out_specs = [pl.BlockSpec((bm, bn), lambda j, i: (i, j))]
    out_shape = [jax.ShapeDtypeStruct((L, d_ff), BF16)]
    args = [hb, w_up_b, w_up_b, conv_w, conv_b]
    if w_in is not None:
        src, dst, shape = _cast_specs(w_in, next_layer, nb * ni, lambda j, i: j * ni + i)
        in_specs.append(src)
        out_specs.append(dst)
        out_shape.append(shape)
        args.append(w_in)
    outs = pl.pallas_call(
        functools.partial(_up_kernel, bm=bm),
        grid=(nb, ni),
        in_specs=in_specs,
        out_specs=out_specs,
        out_shape=out_shape,
        scratch_shapes=[pltpu.VMEM((bm + CONV_PAD, bn), F32)],
        compiler_params=_params("arbitrary", "arbitrary"),
        name="up",
    )(*args)
    return outs if w_in is not None else (outs[0], None)


def kernel(x, w_in, ssm_lam_re, ssm_lam_im, ssm_log_dt, ssm_b_re, ssm_b_im, ssm_c_re, ssm_c_im, ssm_d, w_glu, b_glu, pool_w, pool_scale, w_br_ssm, w_br_pool, w_out, ln1_g, ln1_b, w_up, conv_w, conv_b, w_down, ln2_g, ln2_b):
    bsz, seq, d_model = x.shape
    depth = w_in.shape[0]
    ssm_width = w_glu.shape[1]
    pool_width = pool_scale.shape[1]
    alpha = (2 * depth) ** 0.25
    assert bsz == 1 and seq % (T_CHUNK * 8) == 0 and ssm_width % LANES == 0

    w_in_b, w_glu_b, pool_w_b = w_in[:1].astype(BF16), w_glu.astype(BF16), pool_w.astype(BF16)
    w_br_ssm_b, w_br_pool_b, w_out_b = w_br_ssm.astype(BF16), w_br_pool.astype(BF16), w_out.astype(BF16)
    mats = _s5_matrices(ssm_lam_re, ssm_lam_im, ssm_log_dt, ssm_b_re, ssm_b_im, ssm_c_re, ssm_c_im)
    row = lambda a: a.reshape(depth, 1, -1)
    ssm_d_r, b_glu_r, pool_scale_r, conv_b_r = row(ssm_d), row(b_glu), row(pool_scale), row(conv_b)
    ln1_g_r, ln1_b_r, ln2_g_r, ln2_b_r = row(ln1_g), row(ln1_b), row(ln2_g), row(ln2_b)

    h = x.reshape(seq, d_model)
    hb = h.astype(BF16)
    for layer in range(depth):
        proj, w_up_b, w_down_b = _proj(hb, w_in_b, 0, w_up, w_down, layer)
        yf, yb = _s5(proj, mats, ssm_d_r, layer, ssm_width)
        za = _glu(yf, yb, w_glu_b, b_glu_r, layer)
        zb = _pool(proj, pool_w_b, pool_scale_r, layer, ssm_width, pool_width)
        merged = _merge(za, zb, w_br_ssm_b, w_br_pool_b, proj, layer, ssm_width + pool_width)
        h, hb = _resid_matmul_ln(merged, w_out_b, h, ln1_g_r, ln1_b_r, layer, alpha, "wout")
        if layer + 1 < depth:
            act, w_in_b = _up(hb, w_up_b, conv_w, conv_b_r, layer, w_in, layer + 1)
        else:
            act, _ = _up(hb, w_up_b, conv_w, conv_b_r, layer)
        pre = _resid_matmul(act, w_down_b, h, 0, alpha, DOWN_ROWS, DOWN_COLS, "down")
        h, hb = _layer_norm(pre, ln2_g_r, ln2_b_r, layer)
    return h.reshape(bsz, seq, d_model)
```

```python
import functools
import math

import numpy as np
import jax
import jax.numpy as jnp
from jax import lax
from jax.experimental import pallas as pl
from jax.experimental.pallas import tpu as pltpu

F32 = jnp.float32
BF16 = jnp.bfloat16

SSM_GROUP = 16
SSM_STATE = 64
POOL_WINDOWS = (2, 4, 8, 16)
CONV_WIDTH = 3
LN_EPS = 1e-5

LANES = 128
T_CHUNK = 16
GROUPS_PER_TILE = LANES // SSM_GROUP
HALF_T = LANES // SSM_GROUP
GROUP_COLS = T_CHUNK * SSM_GROUP
POOL_HALO = 16
VMEM_LIMIT = 56 * 1024 * 1024
LN_FUSED_VMEM_LIMIT = 60 * 1024 * 1024
PROJ_VMEM_LIMIT = 60 * 1024 * 1024

ROW_BLOCK = 1024
PROJ_COLS = 1024
EPI_COLS = 512
DOWN_ROWS = 1024
DOWN_COLS = 256
UP_ROWS = 512
UP_COLS = 1024
MXU_COLS = 256
S5_CHUNK_ROWS = 512
SCAN_COLS = 512
POOL_ROWS = 1024
GLU_COLS = 1024
LN_ROWS = 256
LN_FUSED_ROWS = 512


def _gelu(x):
    c = math.sqrt(2.0 / math.pi)
    return x * (0.5 * (1.0 + jnp.tanh(c * (x + 0.044715 * (x * x * x)))))


def _sigmoid(x):
    return 0.5 * jnp.tanh(0.5 * x) + 0.5


def _params(*sem, vmem=VMEM_LIMIT):
    return pltpu.CompilerParams(dimension_semantics=sem, vmem_limit_bytes=vmem)


def _dot(a, b):
    return jnp.dot(a, b, preferred_element_type=F32)


def _cast_specs(w, layer, n_blocks, block_index):
    _, rows, cols = w.shape
    slab = rows // n_blocks
    assert slab * n_blocks == rows and slab % 16 == 0, (rows, n_blocks)
    src = pl.BlockSpec((None, slab, cols), lambda *g: (layer, block_index(*g), 0))
    dst = pl.BlockSpec((None, slab, cols), lambda *g: (0, block_index(*g), 0))
    return src, dst, jax.ShapeDtypeStruct((1, rows, cols), BF16)


def _proj_kernel(x_ref, w_ref, up_src, down_src, o_ref, up_dst, down_dst):
    o_ref[...] = _dot(x_ref[...], w_ref[...])
    up_dst[...] = up_src[...].astype(BF16)
    down_dst[...] = down_src[...].astype(BF16)


def _proj(hb, w_in_b, w_in_layer, w_up, w_down, layer):
    L, D = hb.shape
    n_out = w_in_b.shape[2]
    bm, bn = min(ROW_BLOCK, L), min(PROJ_COLS, n_out)
    gi, gj = L // bm, n_out // bn
    per_row = 1 << (gj.bit_length() - 1)
    block_index = lambda i, j: i * per_row + jnp.minimum(j, per_row - 1)
    up_src, up_dst, up_shape = _cast_specs(w_up, layer, gi * per_row, block_index)
    down_src, down_dst, down_shape = _cast_specs(w_down, layer, gi * per_row, block_index)
    return pl.pallas_call(
        _proj_kernel,
        grid=(gi, gj),
        in_specs=[pl.BlockSpec((bm, D), lambda i, j: (i, 0)),
                  pl.BlockSpec((None, D, bn), lambda i, j: (w_in_layer, 0, j)),
                  up_src, down_src],
        out_specs=[pl.BlockSpec((bm, bn), lambda i, j: (i, j)), up_dst, down_dst],
        out_shape=[jax.ShapeDtypeStruct((L, n_out), F32), up_shape, down_shape],
        compiler_params=_params("arbitrary", "arbitrary", vmem=PROJ_VMEM_LIMIT),
        name="proj",
    )(hb, w_in_b, w_up, w_down)


def _regroup_matrix():
    n = HALF_T * LANES
    src = np.arange(n)
    t8, g8, h = src // LANES, (src % LANES) // SSM_GROUP, src % SSM_GROUP
    dst = g8 * LANES + t8 * SSM_GROUP + h
    m = np.zeros((n, n), np.float32)
    m[src, dst] = 1.0
    return m


def _s5_in_kernel(u_ref, perm_ref, bp_ref, z_ref, sre_ref, sim_ref, *, bc):
    halves = []
    for j in range(T_CHUNK // HALF_T):
        xcat = jnp.concatenate(
            [u_ref[pl.ds(HALF_T * j + t8, bc, stride=T_CHUNK), :].astype(BF16) for t8 in range(HALF_T)],
            axis=1)
        halves.append(_dot(xcat, perm_ref[...]).astype(BF16))
    for g8 in range(GROUPS_PER_TILE):
        for j, half in enumerate(halves):
            z_ref[:, GROUP_COLS * g8 + LANES * j:GROUP_COLS * g8 + LANES * (j + 1)] = (
                half[:, LANES * g8:LANES * (g8 + 1)])
    for q in range(GROUPS_PER_TILE // 2):
        s = _dot(z_ref[:, 2 * GROUP_COLS * q:2 * GROUP_COLS * (q + 1)], bp_ref[q])
        sre_ref[:, LANES * q:LANES * (q + 1)] = s[:, :LANES]
        sim_ref[:, LANES * q:LANES * (q + 1)] = s[:, LANES:]


def _s5_scan_kernel(sre_ref, sim_ref, are_ref, aim_ref, pre_ref, pim_ref):
    n_chunks, width = sre_ref.shape
    ar = are_ref[...]
    ai = aim_ref[...]

    def body(r, carry):
        sr, si = carry
        row = pl.ds(r, 1)
        pre_ref[row, :] = sr
        pim_ref[row, :] = si
        nr = ar * sr - ai * si + sre_ref[row, :]
        ni = ar * si + ai * sr + sim_ref[row, :]
        return nr, ni

    zero = jnp.zeros((1, width), F32)
    lax.fori_loop(0, n_chunks, body, (zero, zero))


def _s5_out_kernel(z_ref, pre_ref, pim_ref, a_ref, cp_ref, permt_ref, u_ref, d_ref,
                   yf_ref, yb_ref, ybuf_ref, *, bc):
    ys = []
    for q in range(GROUPS_PER_TILE // 2):
        state = jnp.concatenate([pre_ref[:, LANES * q:LANES * (q + 1)],
                                 pim_ref[:, LANES * q:LANES * (q + 1)]], axis=1).astype(BF16)
        carried = _dot(state, cp_ref[q])
        for e in range(2):
            g8 = 2 * q + e
            local = _dot(z_ref[:, GROUP_COLS * g8:GROUP_COLS * (g8 + 1)], a_ref[g8])
            ys.append(local + carried[:, GROUP_COLS * e:GROUP_COLS * (e + 1)])
    for j in range(T_CHUNK // HALF_T):
        ycat = jnp.concatenate([y[:, LANES * j:LANES * (j + 1)] for y in ys], axis=1)
        yt = _dot(ycat.astype(BF16), permt_ref[...])
        for t8 in range(HALF_T):
            ybuf_ref[pl.ds(HALF_T * j + t8, bc, stride=T_CHUNK), :] = yt[:, LANES * t8:LANES * (t8 + 1)]
    y = _gelu(ybuf_ref[...] + d_ref[...] * u_ref[...])
    yf_ref[...] = y
    yb_ref[...] = y.astype(BF16)


def _s5_matrices(lam_re, lam_im, log_dt, b_re, b_im, c_re, c_im):
    hp = lax.Precision.HIGHEST
    depth, n_groups, n_state = lam_re.shape
    dt = jnp.exp(log_dt)[..., None]
    zr, zi = lam_re * dt, lam_im * dt
    steps = jnp.arange(T_CHUNK + 1, dtype=F32)
    mag = jnp.exp(zr[..., None] * steps)
    pw_re = mag * jnp.cos(zi[..., None] * steps)
    pw_im = mag * jnp.sin(zi[..., None] * steps)
    e1_re = jnp.expm1(zr) * jnp.cos(zi) - 2.0 * jnp.square(jnp.sin(0.5 * zi))
    e1_im = jnp.exp(zr) * jnp.sin(zi)
    den = lam_re * lam_re + lam_im * lam_im
    f_re = (e1_re * lam_re + e1_im * lam_im) / den
    f_im = (e1_im * lam_re - e1_re * lam_im) / den
    bb_re = f_re[..., None] * b_re - f_im[..., None] * b_im
    bb_im = f_re[..., None] * b_im + f_im[..., None] * b_re
    ct_re, ct_im = jnp.swapaxes(c_re, 2, 3)[:, :, :, None, :], jnp.swapaxes(c_im, 2, 3)[:, :, :, None, :]
    cw_re = ct_re * pw_re[..., None] - ct_im * pw_im[..., None]
    cw_im = ct_re * pw_im[..., None] + ct_im * pw_re[..., None]
    flat = lambda m: m.reshape(depth, n_groups, n_state, GROUP_COLS)
    kern = (jnp.einsum('lgpi,lgpn->lgin', bb_re, flat(cw_re[:, :, :, :T_CHUNK]), precision=hp)
            - jnp.einsum('lgpi,lgpn->lgin', bb_im, flat(cw_im[:, :, :, :T_CHUNK]), precision=hp))
    kpad = jnp.pad(kern.astype(BF16), ((0, 0), (0, 0), (0, 0), (GROUP_COLS, 0)))
    a_mat = jnp.stack([kpad[..., GROUP_COLS - SSM_GROUP * t:2 * GROUP_COLS - SSM_GROUP * t]
                       for t in range(T_CHUNK)], axis=2).reshape(depth, n_groups, GROUP_COLS, GROUP_COLS)
    rev_re = jnp.swapaxes(pw_re[..., T_CHUNK - 1::-1], 2, 3)[:, :, :, None, :]
    rev_im = jnp.swapaxes(pw_im[..., T_CHUNK - 1::-1], 2, 3)[:, :, :, None, :]
    bt_re, bt_im = jnp.swapaxes(bb_re, 2, 3)[:, :, None], jnp.swapaxes(bb_im, 2, 3)[:, :, None]
    pair = lambda m: m.reshape((depth, n_groups // 2, 2) + m.shape[2:])
    bp_re = pair((rev_re * bt_re - rev_im * bt_im).astype(BF16).reshape(depth, n_groups, GROUP_COLS, n_state))
    bp_im = pair((rev_re * bt_im + rev_im * bt_re).astype(BF16).reshape(depth, n_groups, GROUP_COLS, n_state))
    zeros = jnp.zeros_like(bp_re[:, :, 0])
    top = jnp.concatenate([bp_re[:, :, 0], zeros, bp_im[:, :, 0], zeros], axis=-1)
    bot = jnp.concatenate([zeros, bp_re[:, :, 1], zeros, bp_im[:, :, 1]], axis=-1)
    bp = jnp.concatenate([top, bot], axis=-2)
    co_re = pair(flat(cw_re[:, :, :, 1:]).astype(BF16))
    co_im = pair(flat(-cw_im[:, :, :, 1:]).astype(BF16))
    zc = jnp.zeros_like(co_re[:, :, 0])
    cp = jnp.concatenate([
        jnp.concatenate([co_re[:, :, 0], zc], axis=-1),
        jnp.concatenate([zc, co_re[:, :, 1]], axis=-1),
        jnp.concatenate([co_im[:, :, 0], zc], axis=-1),
        jnp.concatenate([zc, co_im[:, :, 1]], axis=-1)], axis=-2)
    a_re = pw_re[..., T_CHUNK].reshape(depth, 1, -1)
    a_im = pw_im[..., T_CHUNK].reshape(depth, 1, -1)
    return a_mat, bp, cp, a_re, a_im


def _s5(proj, mats, ssm_d, layer, ssm_width):
    a_mat, bp, cp, a_re, a_im = mats
    L = proj.shape[0]
    n_groups = ssm_width // SSM_GROUP
    n_tiles = ssm_width // LANES
    n_chunks = L // T_CHUNK
    bc = min(S5_CHUNK_ROWS, n_chunks)
    bm = bc * T_CHUNK
    pairs = GROUPS_PER_TILE // 2
    state_w = n_groups * SSM_STATE
    perm = jnp.asarray(_regroup_matrix(), BF16)
    permt = jnp.asarray(_regroup_matrix().T, BF16)
    n_perm = perm.shape[0]

    z, s_re, s_im = pl.pallas_call(
        functools.partial(_s5_in_kernel, bc=bc),
        grid=(n_tiles, n_chunks // bc),
        in_specs=[pl.BlockSpec((bm, LANES), lambda v, r: (r, v)),
                  pl.BlockSpec((n_perm, n_perm), lambda v, r: (0, 0)),
                  pl.BlockSpec((None, pairs, 2 * GROUP_COLS, 4 * SSM_STATE), lambda v, r: (layer, v, 0, 0))],
        out_specs=[pl.BlockSpec((bc, GROUPS_PER_TILE * GROUP_COLS), lambda v, r: (r, v)),
                   pl.BlockSpec((bc, pairs * LANES), lambda v, r: (r, v)),
                   pl.BlockSpec((bc, pairs * LANES), lambda v, r: (r, v))],
        out_shape=[jax.ShapeDtypeStruct((n_chunks, n_groups * GROUP_COLS), BF16),
                   jax.ShapeDtypeStruct((n_chunks, state_w), F32),
                   jax.ShapeDtypeStruct((n_chunks, state_w), F32)],
        compiler_params=_params("parallel", "parallel"),
        name="s5_in",
    )(proj, perm, bp)

    bw = min(SCAN_COLS, state_w)
    p_re, p_im = pl.pallas_call(
        _s5_scan_kernel,
        grid=(state_w // bw,),
        in_specs=[pl.BlockSpec((n_chunks, bw), lambda c: (0, c)),
                  pl.BlockSpec((n_chunks, bw), lambda c: (0, c)),
                  pl.BlockSpec((None, 1, bw), lambda c: (layer, 0, c)),
                  pl.BlockSpec((None, 1, bw), lambda c: (layer, 0, c))],
        out_specs=[pl.BlockSpec((n_chunks, bw), lambda c: (0, c)),
                   pl.BlockSpec((n_chunks, bw), lambda c: (0, c))],
        out_shape=[jax.ShapeDtypeStruct((n_chunks, state_w), F32),
                   jax.ShapeDtypeStruct((n_chunks, state_w), F32)],
        compiler_params=_params("parallel"),
        name="s5_scan",
    )(s_re, s_im, a_re, a_im)

    yf, yb = pl.pallas_call(
        functools.partial(_s5_out_kernel, bc=bc),
        grid=(n_tiles, n_chunks // bc),
        in_specs=[pl.BlockSpec((bc, GROUPS_PER_TILE * GROUP_COLS), lambda v, r: (r, v)),
                  pl.BlockSpec((bc, pairs * LANES), lambda v, r: (r, v)),
                  pl.BlockSpec((bc, pairs * LANES), lambda v, r: (r, v)),
                  pl.BlockSpec((None, GROUPS_PER_TILE, GROUP_COLS, GROUP_COLS), lambda v, r: (layer, v, 0, 0)),
                  pl.BlockSpec((None, pairs, 4 * SSM_STATE, 2 * GROUP_COLS), lambda v, r: (layer, v, 0, 0)),
                  pl.BlockSpec((n_perm, n_perm), lambda v, r: (0, 0)),
                  pl.BlockSpec((bm, LANES), lambda v, r: (r, v)),
                  pl.BlockSpec((None, 1, LANES), lambda v, r: (layer, 0, v))],
        out_specs=[pl.BlockSpec((bm, LANES), lambda v, r: (r, v)),
                   pl.BlockSpec((bm, LANES), lambda v, r: (r, v))],
        out_shape=[jax.ShapeDtypeStruct((L, ssm_width), F32),
                   jax.ShapeDtypeStruct((L, ssm_width), BF16)],
        scratch_shapes=[pltpu.VMEM((bm, LANES), F32)],
        compiler_params=_params("parallel", "parallel"),
        name="s5_out",
    )(z, p_re, p_im, a_mat, cp, permt, proj, ssm_d)
    return yf, yb


def _glu_kernel(yb_ref, w_ref, b_ref, yf_ref, o_ref):
    gate = _dot(yb_ref[...], w_ref[...]) + b_ref[...]
    o_ref[...] = (yf_ref[...] * _sigmoid(gate)).astype(BF16)


def _glu(yf, yb, w_glu_b, b_glu, layer):
    L, W = yf.shape
    bm, bn = min(ROW_BLOCK, L), min(GLU_COLS, W)
    return pl.pallas_call(
        _glu_kernel,
        grid=(L // bm, W // bn),
        in_specs=[pl.BlockSpec((bm, W), lambda i, j: (i, 0)),
                  pl.BlockSpec((None, W, bn), lambda i, j: (layer, 0, j)),
                  pl.BlockSpec((None, 1, bn), lambda i, j: (layer, 0, j)),
                  pl.BlockSpec((bm, bn), lambda i, j: (i, j))],
        out_specs=pl.BlockSpec((bm, bn), lambda i, j: (i, j)),
        out_shape=jax.ShapeDtypeStruct((L, W), BF16),
        compiler_params=_params("parallel", "parallel"),
        name="glu",
    )(yb, w_glu_b, b_glu, yf)


def _pool_kernel(x_ref, prev_ref, w_ref, sc_ref, o_ref, xe_ref, *, bm, gw):
    i = pl.program_id(0)

    @pl.when(i == 0)
    def _():
        xe_ref[0:POOL_HALO, :] = jnp.zeros((POOL_HALO, xe_ref.shape[1]), F32)

    @pl.when(i > 0)
    def _():
        xe_ref[0:POOL_HALO, :] = prev_ref[...]

    xe_ref[POOL_HALO:POOL_HALO + bm, :] = x_ref[...]
    pos = i * bm + lax.broadcasted_iota(jnp.int32, (bm, 1), 0)
    for gi, win in enumerate(POOL_WINDOWS):
        cols = slice(gw * gi, gw * (gi + 1))
        x = x_ref[:, cols]
        run = xe_ref[:, cols]
        span = 1
        while span < win:
            run = run + pltpu.roll(run, span, axis=0)
            span *= 2
        acc = run[POOL_HALO:, :]
        inv_cnt = 1.0 / jnp.minimum(pos + 1, win).astype(F32)
        pooled = acc * inv_cnt - x
        mixed = _dot(pooled.astype(BF16), w_ref[gi]) * sc_ref[:, cols]
        o_ref[:, cols] = mixed.astype(BF16)


def _pool(proj, pool_w_b, pool_scale, layer, col0, width):
    L = proj.shape[0]
    gw = width // len(POOL_WINDOWS)
    bm = min(POOL_ROWS, L)
    assert col0 % width == 0 and bm % POOL_HALO == 0
    cb = col0 // width
    halo_blocks = bm // POOL_HALO
    return pl.pallas_call(
        functools.partial(_pool_kernel, bm=bm, gw=gw),
        grid=(L // bm,),
        in_specs=[pl.BlockSpec((bm, width), lambda i: (i, cb)),
                  pl.BlockSpec((POOL_HALO, width), lambda i: (jnp.maximum(i * halo_blocks - 1, 0), cb)),
                  pl.BlockSpec((None, len(POOL_WINDOWS), gw, gw), lambda i: (layer, 0, 0, 0)),
                  pl.BlockSpec((None, 1, width), lambda i: (layer, 0, 0))],
        out_specs=pl.BlockSpec((bm, width), lambda i: (i, 0)),
        out_shape=jax.ShapeDtypeStruct((L, width), BF16),
        scratch_shapes=[pltpu.VMEM((bm + POOL_HALO, width), F32)],
        compiler_params=_params("parallel"),
        name="pool",
    )(proj, proj, pool_w_b, pool_scale)


def _merge_kernel(a_ref, b_ref, wa_ref, wb_ref, ga_ref, gb_ref, o_ref):
    ya = _dot(a_ref[...], wa_ref[...])
    yb = _dot(b_ref[...], wb_ref[...])
    o_ref[...] = (_sigmoid(ga_ref[...]) * ya + _sigmoid(gb_ref[...]) * yb).astype(BF16)


def _merge(za, zb, w_a, w_b, proj, layer, gate_col0):
    L, W = za.shape
    D = w_a.shape[2]
    bm, bn = min(ROW_BLOCK, L), min(EPI_COLS, D)
    assert gate_col0 % bn == 0 and D % bn == 0
    ga0 = gate_col0 // bn
    gb0 = (gate_col0 + D) // bn
    return pl.pallas_call(
        _merge_kernel,
        grid=(L // bm, D // bn),
        in_specs=[pl.BlockSpec((bm, W), lambda i, j: (i, 0)),
                  pl.BlockSpec((bm, W), lambda i, j: (i, 0)),
                  pl.BlockSpec((None, W, bn), lambda i, j: (layer, 0, j)),
                  pl.BlockSpec((None, W, bn), lambda i, j: (layer, 0, j)),
                  pl.BlockSpec((bm, bn), lambda i, j: (i, ga0 + j)),
                  pl.BlockSpec((bm, bn), lambda i, j: (i, gb0 + j))],
        out_specs=pl.BlockSpec((bm, bn), lambda i, j: (i, j)),
        out_shape=jax.ShapeDtypeStruct((L, D), BF16),
        compiler_params=_params("parallel", "parallel"),
        name="merge",
    )(za, zb, w_a, w_b, proj, proj)


def _resid_kernel(x_ref, w_ref, h_ref, o_ref, *, alpha):
    o_ref[...] = alpha * h_ref[...] + _dot(x_ref[...], w_ref[...])


def _resid_matmul(xb, w_b, h, layer, alpha, bm, bn, name):
    L, K = xb.shape
    D = w_b.shape[2]
    bm, bn = min(bm, L), min(bn, D)
    return pl.pallas_call(
        functools.partial(_resid_kernel, alpha=alpha),
        grid=(L // bm, D // bn),
        in_specs=[pl.BlockSpec((bm, K), lambda i, j: (i, 0)),
                  pl.BlockSpec((None, K, bn), lambda i, j: (layer, 0, j)),
                  pl.BlockSpec((bm, bn), lambda i, j: (i, j))],
        out_specs=pl.BlockSpec((bm, bn), lambda i, j: (i, j)),
        out_shape=jax.ShapeDtypeStruct((L, D), F32),
        compiler_params=_params("parallel", "parallel"),
        name=name,
    )(xb, w_b, h)


def _resid_ln_kernel(x_ref, w_ref, h_ref, g_ref, b_ref, of_ref, ob_ref, pre_ref, *, alpha, bn):
    j = pl.program_id(1)
    n_col = pre_ref.shape[0]
    pre_ref[j] = alpha * h_ref[...] + _dot(x_ref[...], w_ref[...])

    @pl.when(j == n_col - 1)
    def _():
        inv_d = 1.0 / (n_col * bn)

        total = pre_ref[0].sum(axis=-1, keepdims=True)
        for k in range(1, n_col):
            total = total + pre_ref[k].sum(axis=-1, keepdims=True)
        mu = total * inv_d
        sq = jnp.square(pre_ref[0] - mu).sum(axis=-1, keepdims=True)
        for k in range(1, n_col):
            sq = sq + jnp.square(pre_ref[k] - mu).sum(axis=-1, keepdims=True)
        rstd = lax.rsqrt(sq * inv_d + LN_EPS)
        for k in range(n_col):
            cols = slice(bn * k, bn * (k + 1))
            y = (pre_ref[k] - mu) * rstd * g_ref[:, cols] + b_ref[:, cols]
            of_ref[:, cols] = y
            ob_ref[:, cols] = y.astype(BF16)


def _resid_matmul_ln(xb, w_b, h, g, b, layer, alpha, name):
    L, K = xb.shape
    D = w_b.shape[2]
    bm, bn = min(LN_FUSED_ROWS, L), min(EPI_COLS, D)
    return pl.pallas_call(
        functools.partial(_resid_ln_kernel, alpha=alpha, bn=bn),
        grid=(L // bm, D // bn),
        in_specs=[pl.BlockSpec((bm, K), lambda i, j: (i, 0)),
                  pl.BlockSpec((None, K, bn), lambda i, j: (layer, 0, j)),
                  pl.BlockSpec((bm, bn), lambda i, j: (i, j)),
                  pl.BlockSpec((None, 1, D), lambda i, j: (layer, 0, 0)),
                  pl.BlockSpec((None, 1, D), lambda i, j: (layer, 0, 0))],
        out_specs=[pl.BlockSpec((bm, D), lambda i, j: (i, 0)),
                   pl.BlockSpec((bm, D), lambda i, j: (i, 0))],
        out_shape=[jax.ShapeDtypeStruct((L, D), F32), jax.ShapeDtypeStruct((L, D), BF16)],
        scratch_shapes=[pltpu.VMEM((D // bn, bm, bn), F32)],
        compiler_params=_params("parallel", "arbitrary", vmem=LN_FUSED_VMEM_LIMIT),
        name=name,
    )(xb, w_b, h, g, b)


def _ln_kernel(x_ref, g_ref, b_ref, of_ref, ob_ref):
    x = x_ref[...]
    mu = jnp.mean(x, axis=-1, keepdims=True)
    xc = x - mu
    var = jnp.mean(xc * xc, axis=-1, keepdims=True)
    y = xc * lax.rsqrt(var + LN_EPS) * g_ref[...] + b_ref[...]
    of_ref[...] = y
    ob_ref[...] = y.astype(BF16)


def _layer_norm(x, g, b, layer):
    L, D = x.shape
    bm = min(LN_ROWS, L)
    return pl.pallas_call(
        _ln_kernel,
        grid=(L // bm,),
        in_specs=[pl.BlockSpec((bm, D), lambda i: (i, 0)),
                  pl.BlockSpec((None, 1, D), lambda i: (layer, 0, 0)),
                  pl.BlockSpec((None, 1, D), lambda i: (layer, 0, 0))],
        out_specs=[pl.BlockSpec((bm, D), lambda i: (i, 0)),
                   pl.BlockSpec((bm, D), lambda i: (i, 0))],
        out_shape=[jax.ShapeDtypeStruct((L, D), F32), jax.ShapeDtypeStruct((L, D), BF16)],
        compiler_params=_params("parallel"),
        name="layer_norm",
    )(x, g, b)


CONV_PAD = 8


def _up_kernel(x_ref, wg_ref, wv_ref, cw_ref, cb_ref, *rest, bm):
    if len(rest) == 4:
        next_src, o_ref, next_dst, gbuf_ref = rest
    else:
        (o_ref, gbuf_ref), next_src = rest, None
    i = pl.program_id(1)
    x = x_ref[...]

    @pl.when(i == 0)
    def _():
        gbuf_ref[0:CONV_PAD, :] = jnp.zeros((CONV_PAD, gbuf_ref.shape[1]), F32)

    @pl.when(i > 0)
    def _():
        gbuf_ref[0:CONV_PAD, :] = gbuf_ref[bm:bm + CONV_PAD, :]

    for s in range(o_ref.shape[1] // MXU_COLS):
        cols = slice(MXU_COLS * s, MXU_COLS * (s + 1))
        gate = _dot(x, wg_ref[:, cols])
        val = _dot(x, wv_ref[:, cols])
        gbuf_ref[CONV_PAD:CONV_PAD + bm, cols] = gate
        conv = cw_ref[2:3, cols] * gate + cb_ref[:, cols]
        conv = conv + cw_ref[1:2, cols] * gbuf_ref[CONV_PAD - 1:CONV_PAD - 1 + bm, cols]
        conv = conv + cw_ref[0:1, cols] * gbuf_ref[CONV_PAD - 2:CONV_PAD - 2 + bm, cols]
        o_ref[:, cols] = (_gelu(conv) * val).astype(BF16)
    if next_src is not None:
        next_dst[...] = next_src[...].astype(BF16)


def _up(hb, w_up_b, conv_w, conv_b, layer, w_in=None, next_layer=None):
    L, D = hb.shape
    d_ff = w_up_b.shape[2] // 2
    bm, bn = min(UP_ROWS, L), min(UP_COLS, d_ff)
    nb, ni = d_ff // bn, L // bm
    in_specs = [pl.BlockSpec((bm, D), lambda j, i: (i, 0)),
                pl.BlockSpec((None, D, bn), lambda j, i: (0, 0, j)),
                pl.BlockSpec((None, D, bn), lambda j, i: (0, 0, nb + j)),
                pl.BlockSpec((None, CONV_WIDTH, bn), lambda j, i: (layer, 0, j)),
                pl.BlockSpec((None, 1, bn), lambda j, i: (layer, 0, j))]
    out_specs = [pl.BlockSpec((bm, bn), lambda j, i: (i, j))]
    out_shape = [jax.ShapeDtypeStruct((L, d_ff), BF16)]
    args = [hb, w_up_b, w_up_b, conv_w, conv_b]
    if w_in is not None:
        src, dst, shape = _cast_specs(w_in, next_layer, nb * ni, lambda j, i: j * ni + i)
        in_specs.append(src)
        out_specs.append(dst)
        out_shape.append(shape)
        args.append(w_in)
    outs = pl.pallas_call(
        functools.partial(_up_kernel, bm=bm),
        grid=(nb, ni),
        in_specs=in_specs,
        out_specs=out_specs,
        out_shape=out_shape,
        scratch_shapes=[pltpu.VMEM((bm + CONV_PAD, bn), F32)],
        compiler_params=_params("arbitrary", "arbitrary"),
        name="up",
    )(*args)
    return outs if w_in is not None else (outs[0], None)


def kernel(x, w_in, ssm_lam_re, ssm_lam_im, ssm_log_dt, ssm_b_re, ssm_b_im, ssm_c_re, ssm_c_im, ssm_d, w_glu, b_glu, pool_w, pool_scale, w_br_ssm, w_br_pool, w_out, ln1_g, ln1_b, w_up, conv_w, conv_b, w_down, ln2_g, ln2_b):
    bsz, seq, d_model = x.shape
    depth = w_in.shape[0]
    ssm_width = w_glu.shape[1]
    pool_width = pool_scale.shape[1]
    alpha = (2 * depth) ** 0.25
    assert bsz == 1 and seq % (T_CHUNK * 8) == 0 and ssm_width % LANES == 0

    w_in_b, w_glu_b, pool_w_b = w_in[:1].astype(BF16), w_glu.astype(BF16), pool_w.astype(BF16)
    w_br_ssm_b, w_br_pool_b, w_out_b = w_br_ssm.astype(BF16), w_br_pool.astype(BF16), w_out.astype(BF16)
    mats = _s5_matrices(ssm_lam_re, ssm_lam_im, ssm_log_dt, ssm_b_re, ssm_b_im, ssm_c_re, ssm_c_im)
    row = lambda a: a.reshape(depth, 1, -1)
    ssm_d_r, b_glu_r, pool_scale_r, conv_b_r = row(ssm_d), row(b_glu), row(pool_scale), row(conv_b)
    ln1_g_r, ln1_b_r, ln2_g_r, ln2_b_r = row(ln1_g), row(ln1_b), row(ln2_g), row(ln2_b)

    h = x.reshape(seq, d_model)
    hb = h.astype(BF16)
    for layer in range(depth):
        proj, w_up_b, w_down_b = _proj(hb, w_in_b, 0, w_up, w_down, layer)
        yf, yb = _s5(proj, mats, ssm_d_r, layer, ssm_width)
        za = _glu(yf, yb, w_glu_b, b_glu_r, layer)
        zb = _pool(proj, pool_w_b, pool_scale_r, layer, ssm_width, pool_width)
        merged = _merge(za, zb, w_br_ssm_b, w_br_pool_b, proj, layer, ssm_width + pool_width)
        h, hb = _resid_matmul_ln(merged, w_out_b, h, ln1_g_r, ln1_b_r, layer, alpha, "wout")
        if layer + 1 < depth:
            act, w_in_b = _up(hb, w_up_b, conv_w, conv_b_r, layer, w_in, layer + 1)
        else:
            act, _ = _up(hb, w_up_b, conv_w, conv_b_r, layer)
        pre = _resid_matmul(act, w_down_b, h, 0, alpha, DOWN_ROWS, DOWN_COLS, "down")
        h, hb = _layer_norm(pre, ln2_g_r, ln2_b_r, layer)
    return h.reshape(bsz, seq, d_model)
```
